```python
import jax, jax.numpy as jnp
from jax import lax
import numpy as np

D_MODEL = 1024
BATCH = 16
SEQ = 2048
DEPTH = 1

MEM_LEN = 256
GM_GROUPS = 4
GM_CHUNK = 128
GM_WIDTH = 512
GM_GROUP_DIM = GM_WIDTH // GM_GROUPS
HG_HEADS = 4
HG_KDIM = 128
HG_VDIM = 128
HG_WIDTH = HG_HEADS * HG_KDIM
HG_VWIDTH = HG_HEADS * HG_VDIM
HG_CHUNK = 64
XA_HEADS = 4
XA_HEAD_DIM = 128
XA_WIDTH = XA_HEADS * XA_HEAD_DIM
N_BRANCH = 3
BRANCH_WIDTH = 512
D_FF = 2816
CONV_WIDTH = 3
EPS = 1e-6
IN_SPLITS = (GM_WIDTH, GM_WIDTH, HG_WIDTH, HG_WIDTH, HG_VWIDTH, HG_VWIDTH, XA_WIDTH, N_BRANCH * D_MODEL)
IN_WIDTH = GM_WIDTH * 2 + HG_WIDTH * 2 + HG_VWIDTH * 2 + XA_WIDTH + N_BRANCH * D_MODEL

kernel_name = 'hybrid_gmlp_hgrn2_memattn_convffn'


def rms_norm(x, g):
    xf = x.astype(jnp.float32)
    y = xf * lax.rsqrt(jnp.mean(xf * xf, axis=-1, keepdims=True) + EPS)
    return y.astype(x.dtype) * g


def layer_norm(x, g, b):
    xf = x.astype(jnp.float32)
    mu = jnp.mean(xf, axis=-1, keepdims=True)
    var = jnp.mean(jnp.square(xf - mu), axis=-1, keepdims=True)
    return ((xf - mu) * lax.rsqrt(var + EPS)).astype(x.dtype) * g + b


def spatial_gating(z_u, z_v, ln_g, ln_b, w_s, b_s):
    B, S, _ = z_u.shape
    v = layer_norm(z_v, ln_g, ln_b).reshape(B, S // GM_CHUNK, GM_CHUNK, GM_GROUPS, GM_GROUP_DIM)
    causal = jnp.tril(jnp.ones((GM_CHUNK, GM_CHUNK), dtype=bool))
    w = jnp.where(causal[None], w_s, 0)
    mixed = jnp.einsum('gts,bnsgc->bntgc', w, v) + b_s.T[None, None, :, :, None]
    return z_u * mixed.reshape(B, S, GM_WIDTH)


def hgrn2(q, f_raw, i, g, lb, norm_g):
    B, S, _ = q.shape
    N = S // HG_CHUNK
    f32 = jnp.float32
    lbf = lb.astype(f32).reshape(HG_HEADS, HG_KDIM)
    fg = lbf + (1.0 - lbf) * jax.nn.sigmoid(f_raw.astype(f32).reshape(B, S, HG_HEADS, HG_KDIM))
    k = 1.0 - fg
    logf = jnp.log(fg)
    qf = jax.nn.silu(q.astype(f32)).reshape(B, S, HG_HEADS, HG_KDIM)
    vf = i.astype(f32).reshape(B, S, HG_HEADS, HG_VDIM)

    def to_chunks(t):
        return t.reshape(B, N, HG_CHUNK, HG_HEADS, t.shape[-1]).transpose(1, 0, 3, 2, 4)

    causal = jnp.tril(jnp.ones((HG_CHUNK, HG_CHUNK), dtype=bool))

    def step(state, xs):
        qc, kc, vc, lfc = xs
        a = jnp.cumsum(lfc, axis=2)
        inter = jnp.einsum('bhtk,bhkv->bhtv', qc * jnp.exp(a), state)
        diff = a[:, :, :, None, :] - a[:, :, None, :, :]
        decay = jnp.exp(jnp.where(causal[:, :, None], diff, -jnp.inf))
        scores = jnp.einsum('bhtk,bhtsk,bhsk->bhts', qc, decay, kc)
        intra = jnp.einsum('bhts,bhsv->bhtv', scores, vc)
        a_last = a[:, :, -1:, :]
        new_state = (jnp.exp(a_last[:, :, 0, :])[..., None] * state
                     + jnp.einsum('bhsk,bhsv->bhkv', kc * jnp.exp(a_last - a), vc))
        return new_state, inter + intra

    s0 = jnp.zeros((B, HG_HEADS, HG_KDIM, HG_VDIM), f32)
    _, o = lax.scan(step, s0, (to_chunks(qf), to_chunks(k), to_chunks(vf), to_chunks(logf)))
    o = o.transpose(1, 0, 3, 2, 4).reshape(B, S, HG_HEADS, HG_VDIM)
    o = rms_norm(o, norm_g.astype(f32)) * jax.nn.silu(g.astype(f32).reshape(B, S, HG_HEADS, HG_VDIM))
    return o.reshape(B, S, HG_VWIDTH).astype(q.dtype)


def memory_attention(q, mem, mem_g, w_kv):
    B, S, _ = q.shape
    M = mem.shape[1]
    kv = rms_norm(mem, mem_g) @ w_kv
    k, v = jnp.split(kv, 2, axis=-1)
    k = k.reshape(B, M, XA_HEADS, XA_HEAD_DIM)
    v = v.reshape(B, M, XA_HEADS, XA_HEAD_DIM)
    qh = q.reshape(B, S, XA_HEADS, XA_HEAD_DIM)
    s = jnp.einsum('bshd,bmhd->bhsm', qh, k).astype(jnp.float32) * (XA_HEAD_DIM ** -0.5)
    p = jax.nn.softmax(s, axis=-1).astype(q.dtype)
    return jnp.einsum('bhsm,bmhd->bshd', p, v).reshape(B, S, XA_WIDTH)


def conv_ffn(h, w_up, conv_w, conv_b, w_down):
    S = h.shape[1]
    a, b = jnp.split(h @ w_up, 2, axis=-1)
    ap = jnp.pad(a, ((0, 0), (CONV_WIDTH - 1, 0), (0, 0)))
    ac = conv_b + conv_w[0] * ap[:, 0:S]
    for j in range(1, CONV_WIDTH):
        ac = ac + conv_w[j] * ap[:, j:j + S]
    return (jax.nn.silu(ac) * b) @ w_down


def setup_inputs(seed: int = 0) -> dict:
    key = jax.random.key(seed)
    ks = jax.random.split(key, 24)
    f32 = jnp.float32
    nrm = lambda k, shape, scale: jax.random.normal(k, shape, f32) * scale
    return {
        'x': nrm(ks[0], (BATCH, SEQ, D_MODEL), 1.0),
        'mem': nrm(ks[1], (BATCH, MEM_LEN, D_MODEL), 1.0),
        'norm1_g': 1.0 + nrm(ks[2], (DEPTH, D_MODEL), 0.02),
        'w_in': nrm(ks[3], (DEPTH, D_MODEL, IN_WIDTH), D_MODEL ** -0.5),
        'ln_v_g': 1.0 + nrm(ks[4], (DEPTH, GM_WIDTH), 0.02),
        'ln_v_b': nrm(ks[5], (DEPTH, GM_WIDTH), 0.02),
        'w_spatial': nrm(ks[6], (DEPTH, GM_GROUPS, GM_CHUNK, GM_CHUNK), GM_CHUNK ** -0.5),
        'b_spatial': 1.0 + nrm(ks[7], (DEPTH, GM_GROUPS, GM_CHUNK), 0.02),
        'lb_logits': nrm(ks[8], (DEPTH + 1, HG_WIDTH), 0.5),
        'hgrn_norm_g': 1.0 + nrm(ks[9], (DEPTH, HG_VDIM), 0.02),
        'mem_norm_g': 1.0 + nrm(ks[10], (DEPTH, D_MODEL), 0.02),
        'w_mem_kv': nrm(ks[11], (DEPTH, D_MODEL, 2 * XA_WIDTH), D_MODEL ** -0.5),
        'w_branch': nrm(ks[12], (DEPTH, N_BRANCH, BRANCH_WIDTH, D_MODEL), BRANCH_WIDTH ** -0.5),
        'w_out': nrm(ks[13], (DEPTH, D_MODEL, D_MODEL), D_MODEL ** -0.5),
        'norm2_g': 1.0 + nrm(ks[14], (DEPTH, D_MODEL), 0.02),
        'w_up': nrm(ks[15], (DEPTH, D_MODEL, 2 * D_FF), D_MODEL ** -0.5),
        'conv_w': nrm(ks[16], (DEPTH, CONV_WIDTH, D_FF), CONV_WIDTH ** -0.5),
        'conv_b': nrm(ks[17], (DEPTH, D_FF), 0.02),
        'w_down': nrm(ks[18], (DEPTH, D_FF, D_MODEL), D_FF ** -0.5),
        'final_g': 1.0 + nrm(ks[19], (D_MODEL,), 0.02),
    }


def reference(x, mem, norm1_g, w_in, ln_v_g, ln_v_b, w_spatial, b_spatial, lb_logits,
              hgrn_norm_g, mem_norm_g, w_mem_kv, w_branch, w_out, norm2_g, w_up,
              conv_w, conv_b, w_down, final_g):
    B, S, D = x.shape
    split_points = np.cumsum(IN_SPLITS)[:-1].tolist()
    lb_all = jnp.cumsum(jax.nn.softmax(lb_logits.astype(jnp.float32), axis=0), axis=0)
    for l in range(DEPTH):
        h = rms_norm(x, norm1_g[l])
        proj = h @ w_in[l]
        zu, zv, hq, hf, hi, hg, xq, gate_logits = jnp.split(proj, split_points, axis=-1)
        a_out = spatial_gating(jax.nn.gelu(zu), jax.nn.gelu(zv), ln_v_g[l], ln_v_b[l],
                               w_spatial[l], b_spatial[l])
        b_out = hgrn2(hq, hf, hi, hg, lb_all[l], hgrn_norm_g[l])
        c_out = memory_attention(xq, mem, mem_norm_g[l], w_mem_kv[l])
        branches = jnp.stack([a_out, b_out, c_out], axis=0)
        up = jnp.einsum('nbsc,ncd->nbsd', branches, w_branch[l])
        gates = jax.nn.sigmoid(gate_logits.reshape(B, S, N_BRANCH, D))
        merged = jnp.einsum('bsnd,nbsd->bsd', gates, up)
        x = x + merged @ w_out[l]
        x = x + conv_ffn(rms_norm(x, norm2_g[l]), w_up[l], conv_w[l], conv_b[l], w_down[l])
    return rms_norm(x, final_g)
```

```python
import functools

import jax
import jax.numpy as jnp
from jax import lax
from jax.experimental import pallas as pl
from jax.experimental.pallas import tpu as pltpu

F32 = jnp.float32
BF16 = jnp.bfloat16

EPS = 1e-6
GM_GROUPS = 4
GM_CHUNK = 128
GM_WIDTH = 512
HG_HEADS = 4
HG_DIM = 128
HG_WIDTH = HG_HEADS * HG_DIM
HG_CHUNK = 64
XA_HEADS = 4
XA_DIM = 128
XA_WIDTH = XA_HEADS * XA_DIM
N_BRANCH = 3
BRANCH_WIDTH = 512

SEQ_TILE = 256
V7X_VMEM_LIMIT_BYTES = 56 * 1024 * 1024


def _dot(a, b):
    return jnp.dot(a, b, preferred_element_type=F32)


def _dot_nt(a, b):
    return lax.dot_general(a, b, (((1,), (1,)), ((), ())), preferred_element_type=F32)


def _dot_tn(a, b):
    return lax.dot_general(a, b, (((0,), (0,)), ((), ())), preferred_element_type=F32)


def _sigmoid(x):
    return 0.5 * jnp.tanh(0.5 * x) + 0.5


def _silu(x):
    return x * _sigmoid(x)


def _rms(x):
    return x * lax.rsqrt(jnp.mean(x * x, axis=-1, keepdims=True) + EPS)


def _mixer_kernel(layer, x_ref, mem_ref, g1_ref, w_in_ref, lng_ref, lnb_ref, wsp_ref, bsp_ref,
                  lbl_ref, hng_ref, mg_ref, wkv_ref, wbr_ref, wout_ref, o_ref,
                  kt_sc, v_sc, st_sc, br_sc, ho_sc):
    ts = x_ref.shape[1]
    s_idx = pl.program_id(1)

    @pl.when(s_idx == 0)
    def _start_of_sequence():
        m = mem_ref[0]
        mn = (_rms(m) * mg_ref[...]).astype(BF16)
        kv = _dot(mn, wkv_ref[...])
        kt_sc[...] = kv[:, :XA_WIDTH].T.astype(BF16)
        v_sc[...] = kv[:, XA_WIDTH:].astype(BF16)
        st_sc[...] = jnp.zeros_like(st_sc)

    x = x_ref[0]
    hb = (_rms(x) * g1_ref[...]).astype(BF16)

    def proj(lo, width):
        return _dot(hb, w_in_ref[:, lo:lo + width])

    col = 0

    u = jax.nn.gelu(proj(col, GM_WIDTH)); col += GM_WIDTH
    v = jax.nn.gelu(proj(col, GM_WIDTH)); col += GM_WIDTH
    mu = jnp.mean(v, axis=-1, keepdims=True)
    vc = v - mu
    var = jnp.mean(vc * vc, axis=-1, keepdims=True)
    vn = ((vc * lax.rsqrt(var + EPS)) * lng_ref[...] + lnb_ref[...]).astype(BF16)
    r128 = lax.broadcasted_iota(jnp.int32, (GM_CHUNK, GM_CHUNK), 0)
    c128 = lax.broadcasted_iota(jnp.int32, (GM_CHUNK, GM_CHUNK), 1)
    tril128 = c128 <= r128
    gd = GM_WIDTH // GM_GROUPS
    for g in range(GM_GROUPS):
        w_g = jnp.where(tril128, wsp_ref[g], jnp.zeros((), BF16))
        for c in range(ts // GM_CHUNK):
            rows = slice(c * GM_CHUNK, (c + 1) * GM_CHUNK)
            cols = slice(g * gd, (g + 1) * gd)
            mixed = _dot(w_g, vn[rows, cols]) + bsp_ref[:, cols]
            br_sc[0, rows, cols] = (u[rows, cols] * mixed).astype(BF16)

    hq = proj(col, HG_WIDTH); col += HG_WIDTH
    hf = proj(col, HG_WIDTH); col += HG_WIDTH
    hi = proj(col, HG_WIDTH); col += HG_WIDTH
    hg = proj(col, HG_WIDTH); col += HG_WIDTH
    ll = lbl_ref[...]
    le = jnp.exp(ll - jnp.max(ll, axis=0, keepdims=True))
    lb = jnp.sum(le[:layer + 1], axis=0, keepdims=True) / jnp.sum(le, axis=0, keepdims=True)
    fg = lb + (1.0 - lb) * _sigmoid(hf)
    kk = 1.0 - fg
    logf = jnp.log(fg)
    qf = _silu(hq)
    rt = lax.broadcasted_iota(jnp.int32, (ts, ts), 0)
    ct = lax.broadcasted_iota(jnp.int32, (ts, ts), 1)
    cum_m = jnp.where((ct <= rt) & (rt // HG_CHUNK == ct // HG_CHUNK), 1.0, 0.0).astype(BF16)
    lf_hi = logf.astype(BF16)
    lf_lo = (logf - lf_hi.astype(F32)).astype(BF16)
    a = _dot(cum_m, lf_hi) + _dot(cum_m, lf_lo)
    rc = lax.broadcasted_iota(jnp.int32, (HG_CHUNK, HG_CHUNK), 0)
    cc = lax.broadcasted_iota(jnp.int32, (HG_CHUNK, HG_CHUNK), 1)
    causal = cc <= rc
    vb = hi.astype(BF16)
    for c in range(ts // HG_CHUNK):
        rows = slice(c * HG_CHUNK, (c + 1) * HG_CHUNK)
        a_c = a[rows]
        tot = a_c[HG_CHUNK - 1:HG_CHUNK]
        qa = (qf[rows] * jnp.exp(a_c)).astype(BF16)
        kg = (kk[rows] * jnp.exp(-a_c)).astype(BF16)
        ke = (kk[rows] * jnp.exp(tot - a_c)).astype(BF16)
        dec = jnp.exp(tot)
        for h in range(HG_HEADS):
            cols = slice(h * HG_DIM, (h + 1) * HG_DIM)
            st = st_sc[h]
            sc = jnp.where(causal, _dot_nt(qa[:, cols], kg[:, cols]), 0.0)
            o = _dot(sc.astype(BF16), vb[rows, cols]) + _dot_nt(qa[:, cols], st.astype(BF16))
            ho_sc[rows, cols] = o
            st_sc[h] = st * dec[:, cols] + _dot_tn(vb[rows, cols], ke[:, cols])
    hg_act = _silu(hg)
    for h in range(HG_HEADS):
        cols = slice(h * HG_DIM, (h + 1) * HG_DIM)
        o = ho_sc[:, cols]
        on = _rms(o) * hng_ref[...]
        br_sc[1, :, cols] = (on * hg_act[:, cols]).astype(BF16)

    xq = (proj(col, XA_WIDTH) * (XA_DIM ** -0.5)).astype(BF16); col += XA_WIDTH
    for h in range(XA_HEADS):
        cols = slice(h * XA_DIM, (h + 1) * XA_DIM)
        s = _dot(xq[:, cols], kt_sc[cols, :])
        e = jnp.exp(s - jnp.max(s, axis=-1, keepdims=True))
        inv = 1.0 / jnp.sum(e, axis=-1, keepdims=True)
        br_sc[2, :, cols] = (_dot(e.astype(BF16), v_sc[:, cols]) * inv).astype(BF16)

    d = x.shape[-1]
    merged = None
    for n in range(N_BRANCH):
        gate = _sigmoid(proj(col, d)); col += d
        term = gate * _dot(br_sc[n], wbr_ref[n])
        merged = term if merged is None else merged + term
    o_ref[0] = x + _dot(merged.astype(BF16), wout_ref[...])


def _ffn_kernel(final, x_ref, g2_ref, wup_ref, cw_ref, cb_ref, wdn_ref, gf_ref, o_ref, tail_sc):
    ts = x_ref.shape[1]
    dff = wdn_ref.shape[0]
    s_idx = pl.program_id(1)

    @pl.when(s_idx == 0)
    def _start_of_sequence():
        tail_sc[...] = jnp.zeros_like(tail_sc)

    x = x_ref[0]
    hb = (_rms(x) * g2_ref[...]).astype(BF16)
    a = _dot(hb, wup_ref[:, :dff])
    b = _dot(hb, wup_ref[:, dff:])
    row = lax.broadcasted_iota(jnp.int32, (ts, 1), 0)
    prev1 = tail_sc[7:8, :]
    prev2 = tail_sc[6:7, :]
    a1 = jnp.where(row == 0, prev1, pltpu.roll(a, 1, axis=0))
    a2 = jnp.where(row == 0, prev2, jnp.where(row == 1, prev1, pltpu.roll(a, 2, axis=0)))
    tail_sc[...] = a[ts - 8:, :]
    ac = cb_ref[...] + cw_ref[0:1, :] * a2 + cw_ref[1:2, :] * a1 + cw_ref[2:3, :] * a
    y = (_silu(ac) * b).astype(BF16)
    out = x + _dot(y, wdn_ref[...])
    if final:
        out = _rms(out) * gf_ref[...]
    o_ref[0] = out


def _const_spec(shape):
    return pl.BlockSpec(shape, lambda b, s: (0,) * len(shape), pipeline_mode=pl.Buffered(1))


def _mixer_call(layer, x, mem, g1, w_in, lng, lnb, wsp, bsp, lbl, hng, mg, wkv, wbr, wout):
    bsz, seq, d = x.shape
    mlen = mem.shape[1]
    ts = SEQ_TILE
    consts = (g1, w_in, lng, lnb, wsp, bsp, lbl, hng, mg, wkv, wbr, wout)
    return pl.pallas_call(
        functools.partial(_mixer_kernel, layer),
        grid=(bsz, seq // ts),
        in_specs=[pl.BlockSpec((1, ts, d), lambda b, s: (b, s, 0)),
                  pl.BlockSpec((1, mlen, d), lambda b, s: (b, 0, 0))]
                 + [_const_spec(c.shape) for c in consts],
        out_specs=pl.BlockSpec((1, ts, d), lambda b, s: (b, s, 0)),
        out_shape=jax.ShapeDtypeStruct(x.shape, F32),
        scratch_shapes=[pltpu.VMEM((XA_WIDTH, mlen), BF16),
                        pltpu.VMEM((mlen, XA_WIDTH), BF16),
                        pltpu.VMEM((HG_HEADS, HG_DIM, HG_DIM), F32),
                        pltpu.VMEM((N_BRANCH, ts, BRANCH_WIDTH), BF16),
                        pltpu.VMEM((ts, HG_WIDTH), F32)],
        compiler_params=pltpu.CompilerParams(
            dimension_semantics=("parallel", "arbitrary"),
            vmem_limit_bytes=V7X_VMEM_LIMIT_BYTES),
        name="mixer",
    )(x, mem, *consts)


def _ffn_call(final, x, g2, wup, cw, cb, wdn, gf):
    bsz, seq, d = x.shape
    dff = wdn.shape[0]
    ts = SEQ_TILE
    consts = (g2, wup, cw, cb, wdn, gf)
    return pl.pallas_call(
        functools.partial(_ffn_kernel, final),
        grid=(bsz, seq // ts),
        in_specs=[pl.BlockSpec((1, ts, d), lambda b, s: (b, s, 0))]
                 + [_const_spec(c.shape) for c in consts],
        out_specs=pl.BlockSpec((1, ts, d), lambda b, s: (b, s, 0)),
        out_shape=jax.ShapeDtypeStruct(x.shape, F32),
        scratch_shapes=[pltpu.VMEM((8, dff), F32)],
        compiler_params=pltpu.CompilerParams(
            dimension_semantics=("parallel", "arbitrary"),
            vmem_limit_bytes=V7X_VMEM_LIMIT_BYTES),
        name="ffn",
    )(x, *consts)


def kernel(x, mem, norm1_g, w_in, ln_v_g, ln_v_b, w_spatial, b_spatial, lb_logits, hgrn_norm_g,
           mem_norm_g, w_mem_kv, w_branch, w_out, norm2_g, w_up, conv_w, conv_b, w_down, final_g):
    depth = w_in.shape[0]
    assert x.shape[1] % SEQ_TILE == 0 and SEQ_TILE % GM_CHUNK == 0 and SEQ_TILE % HG_CHUNK == 0
    row = lambda p: p.reshape(1, -1)
    gd = GM_WIDTH // GM_GROUPS
    for l in range(depth):
        bsp = jnp.repeat(b_spatial[l].T, gd, axis=1)
        x = _mixer_call(l, x, mem, row(norm1_g[l]), w_in[l].astype(BF16), row(ln_v_g[l]),
                        row(ln_v_b[l]), w_spatial[l].astype(BF16), bsp, lb_logits,
                        row(hgrn_norm_g[l]), row(mem_norm_g[l]), w_mem_kv[l].astype(BF16),
                        w_branch[l].astype(BF16), w_out[l].astype(BF16))
        x = _ffn_call(l == depth - 1, x, row(norm2_g[l]), w_up[l].astype(BF16), conv_w[l],
                      row(conv_b[l]), w_down[l].astype(BF16), row(final_g))
    return x
```

```python
import functools

import jax
import jax.numpy as jnp
from jax import lax
from jax.experimental import pallas as pl
from jax.experimental.pallas import tpu as pltpu

F32 = jnp.float32
BF16 = jnp.bfloat16

EPS = 1e-6
GM_GROUPS = 4
GM_CHUNK = 128
GM_WIDTH = 512
HG_HEADS = 4
HG_DIM = 128
HG_WIDTH = HG_HEADS * HG_DIM
HG_CHUNK = 64
HG_SUB = 32
XA_HEADS = 4
XA_DIM = 128
XA_WIDTH = XA_HEADS * XA_DIM
N_BRANCH = 3
BRANCH_WIDTH = 512

SEQ_TILE = 256
V7X_VMEM_LIMIT_BYTES = 56 * 1024 * 1024


def _dot(a, b):
    return jnp.dot(a, b, preferred_element_type=F32)


def _dot_nt(a, b):
    return lax.dot_general(a, b, (((1,), (1,)), ((), ())), preferred_element_type=F32)


def _dot_tn(a, b):
    return lax.dot_general(a, b, (((0,), (0,)), ((), ())), preferred_element_type=F32)


def _sigmoid(x):
    return 0.5 * jnp.tanh(0.5 * x) + 0.5


def _silu(x):
    return x * _sigmoid(x)


def _rms(x):
    return x * lax.rsqrt(jnp.mean(x * x, axis=-1, keepdims=True) + EPS)


def _mixer_kernel(layer, x_ref, mem_ref, g1_ref, w_in_ref, lng_ref, lnb_ref, wsp_ref, bsp_ref,
                  lbl_ref, hng_ref, mg_ref, wkv_ref, wbr_ref, wout_ref, o_ref,
                  kt_sc, v_sc, st_sc, br_sc, ho_sc):
    ts = x_ref.shape[1]
    s_idx = pl.program_id(1)

    @pl.when(s_idx == 0)
    def _start_of_sequence():
        m = mem_ref[0]
        mn = (_rms(m) * mg_ref[...]).astype(BF16)
        kv = _dot(mn, wkv_ref[...])
        kt_sc[...] = kv[:, :XA_WIDTH].T.astype(BF16)
        v_sc[...] = kv[:, XA_WIDTH:].astype(BF16)
        st_sc[...] = jnp.zeros_like(st_sc)

    x = x_ref[0]
    hb = (_rms(x) * g1_ref[...]).astype(BF16)

    def proj(lo, width):
        return _dot(hb, w_in_ref[:, lo:lo + width])

    col = 0

    u = jax.nn.gelu(proj(col, GM_WIDTH)); col += GM_WIDTH
    v = jax.nn.gelu(proj(col, GM_WIDTH)); col += GM_WIDTH
    mu = jnp.mean(v, axis=-1, keepdims=True)
    vc = v - mu
    var = jnp.mean(vc * vc, axis=-1, keepdims=True)
    vn = ((vc * lax.rsqrt(var + EPS)) * lng_ref[...] + lnb_ref[...]).astype(BF16)
    r128 = lax.broadcasted_iota(jnp.int32, (GM_CHUNK, GM_CHUNK), 0)
    c128 = lax.broadcasted_iota(jnp.int32, (GM_CHUNK, GM_CHUNK), 1)
    tril128 = c128 <= r128
    gd = GM_WIDTH // GM_GROUPS
    for g in range(GM_GROUPS):
        w_g = jnp.where(tril128, wsp_ref[g], jnp.zeros((), BF16))
        for c in range(ts // GM_CHUNK):
            rows = slice(c * GM_CHUNK, (c + 1) * GM_CHUNK)
            cols = slice(g * gd, (g + 1) * gd)
            mixed = _dot(w_g, vn[rows, cols]) + bsp_ref[:, cols]
            br_sc[0, rows, cols] = (u[rows, cols] * mixed).astype(BF16)

    hq = proj(col, HG_WIDTH); col += HG_WIDTH
    hf = proj(col, HG_WIDTH); col += HG_WIDTH
    hi = proj(col, HG_WIDTH); col += HG_WIDTH
    hg = proj(col, HG_WIDTH); col += HG_WIDTH
    ll = lbl_ref[...]
    le = jnp.exp(ll - jnp.max(ll, axis=0, keepdims=True))
    lb = jnp.sum(le[:layer + 1], axis=0, keepdims=True) / jnp.sum(le, axis=0, keepdims=True)
    fg = lb + (1.0 - lb) * _sigmoid(hf)
    kk = 1.0 - fg
    logf = jnp.log(fg)
    qf = _silu(hq)
    rt = lax.broadcasted_iota(jnp.int32, (ts, ts), 0)
    ct = lax.broadcasted_iota(jnp.int32, (ts, ts), 1)
    cum_m = jnp.where((ct <= rt) & (rt // HG_SUB == ct // HG_SUB), 1.0, 0.0).astype(BF16)
    lf_hi = logf.astype(BF16)
    lf_lo = (logf - lf_hi.astype(F32)).astype(BF16)
    a_sub = _dot(cum_m, lf_hi) + _dot(cum_m, lf_lo)
    rc = lax.broadcasted_iota(jnp.int32, (HG_CHUNK, HG_CHUNK), 0)
    cc = lax.broadcasted_iota(jnp.int32, (HG_CHUNK, HG_CHUNK), 1)
    mask_same = (cc <= rc) & (rc // HG_SUB == cc // HG_SUB)
    mask_later = rc // HG_SUB > cc // HG_SUB
    vb = hi.astype(BF16)
    nsub = HG_CHUNK // HG_SUB
    chunks = []
    for c in range(ts // HG_CHUNK):
        a_blk = [a_sub[c * HG_CHUNK + j * HG_SUB:c * HG_CHUNK + (j + 1) * HG_SUB] for j in range(nsub)]
        tot = [a_j[HG_SUB - 1:HG_SUB] for a_j in a_blk]
        qd, kd, q1, k1, qi, ke = [], [], [], [], [], []
        for j in range(nsub):
            r0 = c * HG_CHUNK + j * HG_SUB
            q_j, k_j, a_j = qf[r0:r0 + HG_SUB], kk[r0:r0 + HG_SUB], a_blk[j]
            z = a_j - a_j[HG_SUB // 2 - 1:HG_SUB // 2]
            qd.append(q_j * jnp.exp(z))
            kd.append(k_j * jnp.exp(-z))
            q1_j = q_j * jnp.exp(a_j)
            k1_j = k_j * jnp.exp(tot[j] - a_j)
            q1.append(q1_j)
            k1.append(k1_j)
            before = sum(tot[:j]) if j else None
            after = sum(tot[j + 1:]) if j + 1 < nsub else None
            qi.append(q1_j if before is None else q1_j * jnp.exp(before))
            ke.append(k1_j if after is None else k1_j * jnp.exp(after))
        cat = lambda parts: jnp.concatenate(parts, axis=0).astype(BF16)
        chunks.append((cat(qd), cat(kd), cat(q1), cat(k1), cat(qi), cat(ke), jnp.exp(sum(tot))))
    pre = {}
    for c, (qd, kd, q1, k1, qi, ke, dec) in enumerate(chunks):
        rows = slice(c * HG_CHUNK, (c + 1) * HG_CHUNK)
        for h in range(HG_HEADS):
            cols = slice(h * HG_DIM, (h + 1) * HG_DIM)
            pre[c, h] = (_dot_nt(qd[:, cols], kd[:, cols]), _dot_nt(q1[:, cols], k1[:, cols]),
                         _dot_tn(vb[rows, cols], ke[:, cols]))
    st = [st_sc[h] for h in range(HG_HEADS)]
    for c, (qd, kd, q1, k1, qi, ke, dec) in enumerate(chunks):
        rows = slice(c * HG_CHUNK, (c + 1) * HG_CHUNK)
        for h in range(HG_HEADS):
            cols = slice(h * HG_DIM, (h + 1) * HG_DIM)
            s_same, s_later, upd = pre[c, h]
            sc = jnp.where(mask_same, s_same, jnp.where(mask_later, s_later, 0.0))
            o = _dot(sc.astype(BF16), vb[rows, cols]) + _dot_nt(qi[:, cols], st[h].astype(BF16))
            ho_sc[rows, cols] = o
            st[h] = st[h] * dec[:, cols] + upd
    for h in range(HG_HEADS):
        st_sc[h] = st[h]
    hg_act = _silu(hg)
    for h in range(HG_HEADS):
        cols = slice(h * HG_DIM, (h + 1) * HG_DIM)
        o = ho_sc[:, cols]
        on = _rms(o) * hng_ref[...]
        br_sc[1, :, cols] = (on * hg_act[:, cols]).astype(BF16)

    xq = (proj(col, XA_WIDTH) * (XA_DIM ** -0.5)).astype(BF16); col += XA_WIDTH
    for h in range(XA_HEADS):
        cols = slice(h * XA_DIM, (h + 1) * XA_DIM)
        s = _dot(xq[:, cols], kt_sc[cols, :])
        e = jnp.exp(s - jnp.max(s, axis=-1, keepdims=True))
        inv = 1.0 / jnp.sum(e, axis=-1, keepdims=True)
        br_sc[2, :, cols] = (_dot(e.astype(BF16), v_sc[:, cols]) * inv).astype(BF16)

    d = x.shape[-1]
    merged = None
    for n in range(N_BRANCH):
        gate = _sigmoid(proj(col, d)); col += d
        term = gate * _dot(br_sc[n], wbr_ref[n])
        merged = term if merged is None else merged + term
    o_ref[0] = x + _dot(merged.astype(BF16), wout_ref[...])


def _ffn_kernel(final, x_ref, g2_ref, wup_ref, cw_ref, cb_ref, wdn_ref, gf_ref, o_ref, tail_sc):
    ts = x_ref.shape[1]
    dff = wdn_ref.shape[0]
    s_idx = pl.program_id(1)

    @pl.when(s_idx == 0)
    def _start_of_sequence():
        tail_sc[...] = jnp.zeros_like(tail_sc)

    x = x_ref[0]
    hb = (_rms(x) * g2_ref[...]).astype(BF16)
    a = _dot(hb, wup_ref[:, :dff])
    b = _dot(hb, wup_ref[:, dff:])
    row = lax.broadcasted_iota(jnp.int32, (ts, 1), 0)
    prev1 = tail_sc[7:8, :]
    prev2 = tail_sc[6:7, :]
    a1 = jnp.where(row == 0, prev1, pltpu.roll(a, 1, axis=0))
    a2 = jnp.where(row == 0, prev2, jnp.where(row == 1, prev1, pltpu.roll(a, 2, axis=0)))
    tail_sc[...] = a[ts - 8:, :]
    ac = cb_ref[...] + cw_ref[0:1, :] * a2 + cw_ref[1:2, :] * a1 + cw_ref[2:3, :] * a
    y = (_silu(ac) * b).astype(BF16)
    out = x + _dot(y, wdn_ref[...])
    if final:
        out = _rms(out) * gf_ref[...]
    o_ref[0] = out


def _const_spec(shape):
    return pl.BlockSpec(shape, lambda b, s: (0,) * len(shape), pipeline_mode=pl.Buffered(1))


def _mixer_call(layer, x, mem, g1, w_in, lng, lnb, wsp, bsp, lbl, hng, mg, wkv, wbr, wout):
    bsz, seq, d = x.shape
    mlen = mem.shape[1]
    ts = SEQ_TILE
    consts = (g1, w_in, lng, lnb, wsp, bsp, lbl, hng, mg, wkv, wbr, wout)
    return pl.pallas_call(
        functools.partial(_mixer_kernel, layer),
        grid=(bsz, seq // ts),
        in_specs=[pl.BlockSpec((1, ts, d), lambda b, s: (b, s, 0)),
                  pl.BlockSpec((1, mlen, d), lambda b, s: (b, 0, 0))]
                 + [_const_spec(c.shape) for c in consts],
        out_specs=pl.BlockSpec((1, ts, d), lambda b, s: (b, s, 0)),
        out_shape=jax.ShapeDtypeStruct(x.shape, F32),
        scratch_shapes=[pltpu.VMEM((XA_WIDTH, mlen), BF16),
                        pltpu.VMEM((mlen, XA_WIDTH), BF16),
                        pltpu.VMEM((HG_HEADS, HG_DIM, HG_DIM), F32),
                        pltpu.VMEM((N_BRANCH, ts, BRANCH_WIDTH), BF16),
                        pltpu.VMEM((ts, HG_WIDTH), F32)],
        compiler_params=pltpu.CompilerParams(
            dimension_semantics=("parallel", "arbitrary"),
            vmem_limit_bytes=V7X_VMEM_LIMIT_BYTES),
        name="mixer",
    )(x, mem, *consts)


def _ffn_call(final, x, g2, wup, cw, cb, wdn, gf):
    bsz, seq, d = x.shape
    dff = wdn.shape[0]
    ts = SEQ_TILE
    consts = (g2, wup, cw, cb, wdn, gf)
    return pl.pallas_call(
        functools.partial(_ffn_kernel, final),
        grid=(bsz, seq // ts),
        in_specs=[pl.BlockSpec((1, ts, d), lambda b, s: (b, s, 0))]
                 + [_const_spec(c.shape) for c in consts],
        out_specs=pl.BlockSpec((1, ts, d), lambda b, s: (b, s, 0)),
        out_shape=jax.ShapeDtypeStruct(x.shape, F32),
        scratch_shapes=[pltpu.VMEM((8, dff), F32)],
        compiler_params=pltpu.CompilerParams(
            dimension_semantics=("parallel", "arbitrary"),
            vmem_limit_bytes=V7X_VMEM_LIMIT_BYTES),
        name="ffn",
    )(x, *consts)


def kernel(x, mem, norm1_g, w_in, ln_v_g, ln_v_b, w_spatial, b_spatial, lb_logits, hgrn_norm_g,
           mem_norm_g, w_mem_kv, w_branch, w_out, norm2_g, w_up, conv_w, conv_b, w_down, final_g):
    depth = w_in.shape[0]
    assert x.shape[1] % SEQ_TILE == 0 and SEQ_TILE % GM_CHUNK == 0 and SEQ_TILE % HG_CHUNK == 0
    assert HG_CHUNK % HG_SUB == 0 and HG_SUB % 16 == 0
    row = lambda p: p.reshape(1, -1)
    gd = GM_WIDTH // GM_GROUPS
    for l in range(depth):
        bsp = jnp.repeat(b_spatial[l].T, gd, axis=1)
        x = _mixer_call(l, x, mem, row(norm1_g[l]), w_in[l].astype(BF16), row(ln_v_g[l]),
                        row(ln_v_b[l]), w_spatial[l].astype(BF16), bsp, lb_logits,
                        row(hgrn_norm_g[l]), row(mem_norm_g[l]), w_mem_kv[l].astype(BF16),
                        w_branch[l].astype(BF16), w_out[l].astype(BF16))
        x = _ffn_call(l == depth - 1, x, row(norm2_g[l]), w_up[l].astype(BF16), conv_w[l],
                      row(conv_b[l]), w_down[l].astype(BF16), row(final_g))
    return x
```

```python
import functools

import jax
import jax.numpy as jnp
from jax import lax
from jax.experimental import pallas as pl
from jax.experimental.pallas import tpu as pltpu

F32 = jnp.float32
BF16 = jnp.bfloat16

EPS = 1e-6
GM_GROUPS = 4
GM_CHUNK = 128
GM_WIDTH = 512
HG_HEADS = 4
HG_DIM = 128
HG_WIDTH = HG_HEADS * HG_DIM
HG_CHUNK = 64
HG_SUB = 32
XA_HEADS = 4
XA_DIM = 128
XA_WIDTH = XA_HEADS * XA_DIM
N_BRANCH = 3
BRANCH_WIDTH = 512

SEQ_TILE = 256
V7X_VMEM_LIMIT_BYTES = 56 * 1024 * 1024


def _dot(a, b):
    return jnp.dot(a, b, preferred_element_type=F32)


def _dot_nt(a, b):
    return lax.dot_general(a, b, (((1,), (1,)), ((), ())), preferred_element_type=F32)


def _dot_tn(a, b):
    return lax.dot_general(a, b, (((0,), (0,)), ((), ())), preferred_element_type=F32)


def _sigmoid(x):
    return 0.5 * jnp.tanh(0.5 * x) + 0.5


def _silu(x):
    return x * _sigmoid(x)


def _rms(x):
    return x * lax.rsqrt(jnp.mean(x * x, axis=-1, keepdims=True) + EPS)


def _mixer_kernel(layer, x_ref, mem_ref, g1_ref, w_in_ref, lng_ref, lnb_ref, wsp_ref, bsp_ref,
                  lbl_ref, hng_ref, mg_ref, wkv_ref, wbr_ref, wout_ref, o_ref,
                  kt_sc, v_sc, st_sc, br_sc, ho_sc):
    ts = x_ref.shape[1]
    s_idx = pl.program_id(1)

    @pl.when(s_idx == 0)
    def _start_of_sequence():
        m = mem_ref[0]
        mn = (_rms(m) * mg_ref[...]).astype(BF16)
        kv = _dot(mn, wkv_ref[...])
        kt_sc[...] = kv[:, :XA_WIDTH].T.astype(BF16)
        v_sc[...] = kv[:, XA_WIDTH:].astype(BF16)
        st_sc[...] = jnp.zeros_like(st_sc)

    x = x_ref[0]
    hb = (_rms(x) * g1_ref[...]).astype(BF16)
    d = x.shape[-1]

    def proj(lo, width):
        return _dot(hb, w_in_ref[:, lo:lo + width])

    c_u, c_v = 0, GM_WIDTH
    c_q = 2 * GM_WIDTH
    c_f, c_i, c_g = c_q + HG_WIDTH, c_q + 2 * HG_WIDTH, c_q + 3 * HG_WIDTH
    c_x = c_q + 4 * HG_WIDTH
    c_gate = c_x + XA_WIDTH

    p_u = proj(c_u, GM_WIDTH)
    p_v = proj(c_v, GM_WIDTH)
    hq = proj(c_q, HG_WIDTH)
    hf = proj(c_f, HG_WIDTH)
    hi = proj(c_i, HG_WIDTH)
    hg = proj(c_g, HG_WIDTH)

    u = jax.nn.gelu(p_u)
    v = jax.nn.gelu(p_v)
    mu = jnp.mean(v, axis=-1, keepdims=True)
    vc = v - mu
    var = jnp.mean(vc * vc, axis=-1, keepdims=True)
    vn = ((vc * lax.rsqrt(var + EPS)) * lng_ref[...] + lnb_ref[...]).astype(BF16)
    p_x = proj(c_x, XA_WIDTH)
    r128 = lax.broadcasted_iota(jnp.int32, (GM_CHUNK, GM_CHUNK), 0)
    c128 = lax.broadcasted_iota(jnp.int32, (GM_CHUNK, GM_CHUNK), 1)
    tril128 = c128 <= r128
    gd = GM_WIDTH // GM_GROUPS
    for g in range(GM_GROUPS):
        w_g = jnp.where(tril128, wsp_ref[g], jnp.zeros((), BF16))
        for c in range(ts // GM_CHUNK):
            rows = slice(c * GM_CHUNK, (c + 1) * GM_CHUNK)
            cols = slice(g * gd, (g + 1) * gd)
            mixed = _dot(w_g, vn[rows, cols]) + bsp_ref[:, cols]
            br_sc[0, rows, cols] = (u[rows, cols] * mixed).astype(BF16)
    gl = [proj(c_gate, d)]

    ll = lbl_ref[...]
    le = jnp.exp(ll - jnp.max(ll, axis=0, keepdims=True))
    lb = jnp.sum(le[:layer + 1], axis=0, keepdims=True) / jnp.sum(le, axis=0, keepdims=True)
    fg = lb + (1.0 - lb) * _sigmoid(hf)
    kk = 1.0 - fg
    logf = jnp.log(fg)
    qf = _silu(hq)
    rt = lax.broadcasted_iota(jnp.int32, (ts, ts), 0)
    ct = lax.broadcasted_iota(jnp.int32, (ts, ts), 1)
    cum_m = jnp.where((ct <= rt) & (rt // HG_SUB == ct // HG_SUB), 1.0, 0.0).astype(BF16)
    lf_hi = logf.astype(BF16)
    lf_lo = (logf - lf_hi.astype(F32)).astype(BF16)
    a_sub = _dot(cum_m, lf_hi) + _dot(cum_m, lf_lo)

    xq = (p_x * (XA_DIM ** -0.5)).astype(BF16)
    att_s = [_dot(xq[:, h * XA_DIM:(h + 1) * XA_DIM], kt_sc[h * XA_DIM:(h + 1) * XA_DIM, :])
             for h in range(XA_HEADS)]
    gl.append(proj(c_gate + d, d))

    rc = lax.broadcasted_iota(jnp.int32, (HG_CHUNK, HG_CHUNK), 0)
    cc = lax.broadcasted_iota(jnp.int32, (HG_CHUNK, HG_CHUNK), 1)
    mask_same = (cc <= rc) & (rc // HG_SUB == cc // HG_SUB)
    mask_later = rc // HG_SUB > cc // HG_SUB
    vb = hi.astype(BF16)
    nsub = HG_CHUNK // HG_SUB
    chunks = []
    for c in range(ts // HG_CHUNK):
        a_blk = [a_sub[c * HG_CHUNK + j * HG_SUB:c * HG_CHUNK + (j + 1) * HG_SUB] for j in range(nsub)]
        tot = [a_j[HG_SUB - 1:HG_SUB] for a_j in a_blk]
        qd, kd, q1, k1, qi, ke = [], [], [], [], [], []
        for j in range(nsub):
            r0 = c * HG_CHUNK + j * HG_SUB
            q_j, k_j, a_j = qf[r0:r0 + HG_SUB], kk[r0:r0 + HG_SUB], a_blk[j]
            z = a_j - a_j[HG_SUB // 2 - 1:HG_SUB // 2]
            qd.append(q_j * jnp.exp(z))
            kd.append(k_j * jnp.exp(-z))
            q1_j = q_j * jnp.exp(a_j)
            k1_j = k_j * jnp.exp(tot[j] - a_j)
            q1.append(q1_j)
            k1.append(k1_j)
            before = sum(tot[:j]) if j else None
            after = sum(tot[j + 1:]) if j + 1 < nsub else None
            qi.append(q1_j if before is None else q1_j * jnp.exp(before))
            ke.append(k1_j if after is None else k1_j * jnp.exp(after))
        cat = lambda parts: jnp.concatenate(parts, axis=0).astype(BF16)
        chunks.append((cat(qd), cat(kd), cat(q1), cat(k1), cat(qi), cat(ke), jnp.exp(sum(tot))))
    pre = {}
    for c, (qd, kd, q1, k1, qi, ke, dec) in enumerate(chunks):
        rows = slice(c * HG_CHUNK, (c + 1) * HG_CHUNK)
        for h in range(HG_HEADS):
            cols = slice(h * HG_DIM, (h + 1) * HG_DIM)
            pre[c, h] = (_dot_nt(qd[:, cols], kd[:, cols]), _dot_nt(q1[:, cols], k1[:, cols]),
                         _dot_tn(vb[rows, cols], ke[:, cols]))

    for h in range(XA_HEADS):
        cols = slice(h * XA_DIM, (h + 1) * XA_DIM)
        s = att_s[h]
        e = jnp.exp(s - jnp.max(s, axis=-1, keepdims=True))
        inv = 1.0 / jnp.sum(e, axis=-1, keepdims=True)
        br_sc[2, :, cols] = (_dot(e.astype(BF16), v_sc[:, cols]) * inv).astype(BF16)
    gl.append(proj(c_gate + 2 * d, d))

    st = [st_sc[h] for h in range(HG_HEADS)]
    for c, (qd, kd, q1, k1, qi, ke, dec) in enumerate(chunks):
        rows = slice(c * HG_CHUNK, (c + 1) * HG_CHUNK)
        for h in range(HG_HEADS):
            cols = slice(h * HG_DIM, (h + 1) * HG_DIM)
            s_same, s_later, upd = pre[c, h]
            sc = jnp.where(mask_same, s_same, jnp.where(mask_later, s_later, 0.0))
            o = _dot(sc.astype(BF16), vb[rows, cols]) + _dot_nt(qi[:, cols], st[h].astype(BF16))
            ho_sc[rows, cols] = o
            st[h] = st[h] * dec[:, cols] + upd
    for h in range(HG_HEADS):
        st_sc[h] = st[h]
    hg_act = _silu(hg)
    for h in range(HG_HEADS):
        cols = slice(h * HG_DIM, (h + 1) * HG_DIM)
        o = ho_sc[:, cols]
        on = _rms(o) * hng_ref[...]
        br_sc[1, :, cols] = (on * hg_act[:, cols]).astype(BF16)

    merged = None
    for n in range(N_BRANCH):
        term = _sigmoid(gl[n]) * _dot(br_sc[n], wbr_ref[n])
        merged = term if merged is None else merged + term
    o_ref[0] = x + _dot(merged.astype(BF16), wout_ref[...])


def _ffn_kernel(final, x_ref, g2_ref, wup_ref, cw_ref, cb_ref, wdn_ref, gf_ref, o_ref, tail_sc):
    ts = x_ref.shape[1]
    dff = wdn_ref.shape[0]
    s_idx = pl.program_id(1)

    @pl.when(s_idx == 0)
    def _start_of_sequence():
        tail_sc[...] = jnp.zeros_like(tail_sc)

    x = x_ref[0]
    hb = (_rms(x) * g2_ref[...]).astype(BF16)
    a = _dot(hb, wup_ref[:, :dff])
    b = _dot(hb, wup_ref[:, dff:])
    row = lax.broadcasted_iota(jnp.int32, (ts, 1), 0)
    prev1 = tail_sc[7:8, :]
    prev2 = tail_sc[6:7, :]
    a1 = jnp.where(row == 0, prev1, pltpu.roll(a, 1, axis=0))
    a2 = jnp.where(row == 0, prev2, jnp.where(row == 1, prev1, pltpu.roll(a, 2, axis=0)))
    tail_sc[...] = a[ts - 8:, :]
    ac = cb_ref[...] + cw_ref[0:1, :] * a2 + cw_ref[1:2, :] * a1 + cw_ref[2:3, :] * a
    y = (_silu(ac) * b).astype(BF16)
    out = x + _dot(y, wdn_ref[...])
    if final:
        out = _rms(out) * gf_ref[...]
    o_ref[0] = out


def _const_spec(shape):
    return pl.BlockSpec(shape, lambda b, s: (0,) * len(shape), pipeline_mode=pl.Buffered(1))


def _mixer_call(layer, x, mem, g1, w_in, lng, lnb, wsp, bsp, lbl, hng, mg, wkv, wbr, wout):
    bsz, seq, d = x.shape
    mlen = mem.shape[1]
    ts = SEQ_TILE
    consts = (g1, w_in, lng, lnb, wsp, bsp, lbl, hng, mg, wkv, wbr, wout)
    return pl.pallas_call(
        functools.partial(_mixer_kernel, layer),
        grid=(bsz, seq // ts),
        in_specs=[pl.BlockSpec((1, ts, d), lambda b, s: (b, s, 0)),
                  pl.BlockSpec((1, mlen, d), lambda b, s: (b, 0, 0))]
                 + [_const_spec(c.shape) for c in consts],
        out_specs=pl.BlockSpec((1, ts, d), lambda b, s: (b, s, 0)),
        out_shape=jax.ShapeDtypeStruct(x.shape, F32),
        scratch_shapes=[pltpu.VMEM((XA_WIDTH, mlen), BF16),
                        pltpu.VMEM((mlen, XA_WIDTH), BF16),
                        pltpu.VMEM((HG_HEADS, HG_DIM, HG_DIM), F32),
                        pltpu.VMEM((N_BRANCH, ts, BRANCH_WIDTH), BF16),
                        pltpu.VMEM((ts, HG_WIDTH), F32)],
        compiler_params=pltpu.CompilerParams(
            dimension_semantics=("parallel", "arbitrary"),
            vmem_limit_bytes=V7X_VMEM_LIMIT_BYTES),
        name="mixer",
    )(x, mem, *consts)


def _ffn_call(final, x, g2, wup, cw, cb, wdn, gf):
    bsz, seq, d = x.shape
    dff = wdn.shape[0]
    ts = SEQ_TILE
    consts = (g2, wup, cw, cb, wdn, gf)
    return pl.pallas_call(
        functools.partial(_ffn_kernel, final),
        grid=(bsz, seq // ts),
        in_specs=[pl.BlockSpec((1, ts, d), lambda b, s: (b, s, 0))]
                 + [_const_spec(c.shape) for c in consts],
        out_specs=pl.BlockSpec((1, ts, d), lambda b, s: (b, s, 0)),
        out_shape=jax.ShapeDtypeStruct(x.shape, F32),
        scratch_shapes=[pltpu.VMEM((8, dff), F32)],
        compiler_params=pltpu.CompilerParams(
            dimension_semantics=("parallel", "arbitrary"),
            vmem_limit_bytes=V7X_VMEM_LIMIT_BYTES),
        name="ffn",
    )(x, *consts)


def kernel(x, mem, norm1_g, w_in, ln_v_g, ln_v_b, w_spatial, b_spatial, lb_logits, hgrn_norm_g,
           mem_norm_g, w_mem_kv, w_branch, w_out, norm2_g, w_up, conv_w, conv_b, w_down, final_g):
    depth = w_in.shape[0]
    assert x.shape[1] % SEQ_TILE == 0 and SEQ_TILE % GM_CHUNK == 0 and SEQ_TILE % HG_CHUNK == 0
    assert HG_CHUNK % HG_SUB == 0 and HG_SUB % 16 == 0
    row = lambda p: p.reshape(1, -1)
    gd = GM_WIDTH // GM_GROUPS
    for l in range(depth):
        bsp = jnp.repeat(b_spatial[l].T, gd, axis=1)
        x = _mixer_call(l, x, mem, row(norm1_g[l]), w_in[l].astype(BF16), row(ln_v_g[l]),
                        row(ln_v_b[l]), w_spatial[l].astype(BF16), bsp, lb_logits,
                        row(hgrn_norm_g[l]), row(mem_norm_g[l]), w_mem_kv[l].astype(BF16),
                        w_branch[l].astype(BF16), w_out[l].astype(BF16))
        x = _ffn_call(l == depth - 1, x, row(norm2_g[l]), w_up[l].astype(BF16), conv_w[l],
                      row(conv_b[l]), w_down[l].astype(BF16), row(final_g))
    return x
```

```python
import functools

import jax
import jax.numpy as jnp
from jax import lax
from jax.experimental import pallas as pl
from jax.experimental.pallas import tpu as pltpu

F32 = jnp.float32
BF16 = jnp.bfloat16

EPS = 1e-6
GM_GROUPS = 4
GM_CHUNK = 128
GM_WIDTH = 512
HG_HEADS = 4
HG_DIM = 128
HG_WIDTH = HG_HEADS * HG_DIM
HG_CHUNK = 64
HG_SUB = 32
XA_HEADS = 4
XA_DIM = 128
XA_WIDTH = XA_HEADS * XA_DIM
N_BRANCH = 3
BRANCH_WIDTH = 512

SEQ_TILE = 512
V7X_VMEM_LIMIT_BYTES = 56 * 1024 * 1024
V7X_MXU_DEPTH = 256


def _dot(a, b):
    return jnp.dot(a, b, preferred_element_type=F32)


def _dot_nt(a, b):
    return lax.dot_general(a, b, (((1,), (1,)), ((), ())), preferred_element_type=F32)


def _dot_tn(a, b):
    return lax.dot_general(a, b, (((0,), (0,)), ((), ())), preferred_element_type=F32)


def _sigmoid(x):
    return 0.5 * jnp.tanh(0.5 * x) + 0.5


def _silu(x):
    return x * _sigmoid(x)


def _rms(x):
    return x * lax.rsqrt(jnp.mean(x * x, axis=-1, keepdims=True) + EPS)


def _mixer_kernel(layer, x_ref, mem_ref, g1_ref, w_in_ref, lng_ref, lnb_ref, wsp_ref, bsp_ref,
                  lbl_ref, hng_ref, mg_ref, wkv_ref, wbr_ref, wout_ref, o_ref,
                  kt_sc, v_sc, st_sc, br_sc, ho_sc):
    ts = x_ref.shape[1]
    s_idx = pl.program_id(1)

    @pl.when(s_idx == 0)
    def _start_of_sequence():
        m = mem_ref[0]
        mn = (_rms(m) * mg_ref[...]).astype(BF16)
        kv = _dot(mn, wkv_ref[...])
        kt_sc[...] = kv[:, :XA_WIDTH].T.astype(BF16)
        v_sc[...] = kv[:, XA_WIDTH:].astype(BF16)
        st_sc[...] = jnp.zeros_like(st_sc)

    x = x_ref[0]
    hb = (_rms(x) * g1_ref[...]).astype(BF16)
    d = x.shape[-1]

    def proj(lo, width):
        return _dot(hb, w_in_ref[:, lo:lo + width])

    c_u, c_v = 0, GM_WIDTH
    c_q = 2 * GM_WIDTH
    c_f, c_i, c_g = c_q + HG_WIDTH, c_q + 2 * HG_WIDTH, c_q + 3 * HG_WIDTH
    c_x = c_q + 4 * HG_WIDTH
    c_gate = c_x + XA_WIDTH

    p_u = proj(c_u, GM_WIDTH)
    p_v = proj(c_v, GM_WIDTH)
    hq = proj(c_q, HG_WIDTH)
    hf = proj(c_f, HG_WIDTH)
    hi = proj(c_i, HG_WIDTH)
    hg = proj(c_g, HG_WIDTH)

    u = jax.nn.gelu(p_u)
    v = jax.nn.gelu(p_v)
    mu = jnp.mean(v, axis=-1, keepdims=True)
    vc = v - mu
    var = jnp.mean(vc * vc, axis=-1, keepdims=True)
    vn = ((vc * lax.rsqrt(var + EPS)) * lng_ref[...] + lnb_ref[...]).astype(BF16)
    p_x = proj(c_x, XA_WIDTH)
    r128 = lax.broadcasted_iota(jnp.int32, (GM_CHUNK, GM_CHUNK), 0)
    c128 = lax.broadcasted_iota(jnp.int32, (GM_CHUNK, GM_CHUNK), 1)
    tril128 = c128 <= r128
    gd = GM_WIDTH // GM_GROUPS
    for g in range(GM_GROUPS):
        w_g = jnp.where(tril128, wsp_ref[g], jnp.zeros((), BF16))
        for c in range(ts // GM_CHUNK):
            rows = slice(c * GM_CHUNK, (c + 1) * GM_CHUNK)
            cols = slice(g * gd, (g + 1) * gd)
            mixed = _dot(w_g, vn[rows, cols]) + bsp_ref[:, cols]
            br_sc[0, rows, cols] = (u[rows, cols] * mixed).astype(BF16)
    gl = [proj(c_gate, d)]

    ll = lbl_ref[...]
    le = jnp.exp(ll - jnp.max(ll, axis=0, keepdims=True))
    lb = jnp.sum(le[:layer + 1], axis=0, keepdims=True) / jnp.sum(le, axis=0, keepdims=True)
    fg = lb + (1.0 - lb) * _sigmoid(hf)
    kk = 1.0 - fg
    logf = jnp.log(fg)
    qf = _silu(hq)
    cr = min(ts, V7X_MXU_DEPTH)
    rt = lax.broadcasted_iota(jnp.int32, (cr, cr), 0)
    ct = lax.broadcasted_iota(jnp.int32, (cr, cr), 1)
    cum_m = jnp.where((ct <= rt) & (rt // HG_SUB == ct // HG_SUB), 1.0, 0.0).astype(BF16)
    lf_hi = logf.astype(BF16)
    lf_lo = (logf - lf_hi.astype(F32)).astype(BF16)
    a_sub = jnp.concatenate(
        [_dot(cum_m, lf_hi[r:r + cr]) + _dot(cum_m, lf_lo[r:r + cr]) for r in range(0, ts, cr)],
        axis=0)

    xq = (p_x * (XA_DIM ** -0.5)).astype(BF16)
    att_s = [_dot(xq[:, h * XA_DIM:(h + 1) * XA_DIM], kt_sc[h * XA_DIM:(h + 1) * XA_DIM, :])
             for h in range(XA_HEADS)]
    gl.append(proj(c_gate + d, d))

    rc = lax.broadcasted_iota(jnp.int32, (HG_CHUNK, HG_CHUNK), 0)
    cc = lax.broadcasted_iota(jnp.int32, (HG_CHUNK, HG_CHUNK), 1)
    mask_same = (cc <= rc) & (rc // HG_SUB == cc // HG_SUB)
    mask_later = rc // HG_SUB > cc // HG_SUB
    vb = hi.astype(BF16)
    nsub = HG_CHUNK // HG_SUB
    chunks = []
    for c in range(ts // HG_CHUNK):
        a_blk = [a_sub[c * HG_CHUNK + j * HG_SUB:c * HG_CHUNK + (j + 1) * HG_SUB] for j in range(nsub)]
        tot = [a_j[HG_SUB - 1:HG_SUB] for a_j in a_blk]
        qd, kd, q1, k1, qi, ke = [], [], [], [], [], []
        for j in range(nsub):
            r0 = c * HG_CHUNK + j * HG_SUB
            q_j, k_j, a_j = qf[r0:r0 + HG_SUB], kk[r0:r0 + HG_SUB], a_blk[j]
            z = a_j - a_j[HG_SUB // 2 - 1:HG_SUB // 2]
            qd.append(q_j * jnp.exp(z))
            kd.append(k_j * jnp.exp(-z))
            q1_j = q_j * jnp.exp(a_j)
            k1_j = k_j * jnp.exp(tot[j] - a_j)
            q1.append(q1_j)
            k1.append(k1_j)
            before = sum(tot[:j]) if j else None
            after = sum(tot[j + 1:]) if j + 1 < nsub else None
            qi.append(q1_j if before is None else q1_j * jnp.exp(before))
            ke.append(k1_j if after is None else k1_j * jnp.exp(after))
        cat = lambda parts: jnp.concatenate(parts, axis=0).astype(BF16)
        chunks.append((cat(qd), cat(kd), cat(q1), cat(k1), cat(qi), cat(ke), jnp.exp(sum(tot))))
    pre = {}
    for c, (qd, kd, q1, k1, qi, ke, dec) in enumerate(chunks):
        rows = slice(c * HG_CHUNK, (c + 1) * HG_CHUNK)
        for h in range(HG_HEADS):
            cols = slice(h * HG_DIM, (h + 1) * HG_DIM)
            pre[c, h] = (_dot_nt(qd[:, cols], kd[:, cols]), _dot_nt(q1[:, cols], k1[:, cols]),
                         _dot_tn(vb[rows, cols], ke[:, cols]))

    for h in range(XA_HEADS):
        cols = slice(h * XA_DIM, (h + 1) * XA_DIM)
        s = att_s[h]
        e = jnp.exp(s - jnp.max(s, axis=-1, keepdims=True))
        inv = 1.0 / jnp.sum(e, axis=-1, keepdims=True)
        br_sc[2, :, cols] = (_dot(e.astype(BF16), v_sc[:, cols]) * inv).astype(BF16)
    gl.append(proj(c_gate + 2 * d, d))

    st = [st_sc[h] for h in range(HG_HEADS)]
    for c, (qd, kd, q1, k1, qi, ke, dec) in enumerate(chunks):
        rows = slice(c * HG_CHUNK, (c + 1) * HG_CHUNK)
        for h in range(HG_HEADS):
            cols = slice(h * HG_DIM, (h + 1) * HG_DIM)
            s_same, s_later, upd = pre[c, h]
            sc = jnp.where(mask_same, s_same, jnp.where(mask_later, s_later, 0.0))
            o = _dot(sc.astype(BF16), vb[rows, cols]) + _dot_nt(qi[:, cols], st[h].astype(BF16))
            ho_sc[rows, cols] = o
            st[h] = st[h] * dec[:, cols] + upd
    for h in range(HG_HEADS):
        st_sc[h] = st[h]
    hg_act = _silu(hg)
    for h in range(HG_HEADS):
        cols = slice(h * HG_DIM, (h + 1) * HG_DIM)
        o = ho_sc[:, cols]
        on = _rms(o) * hng_ref[...]
        br_sc[1, :, cols] = (on * hg_act[:, cols]).astype(BF16)

    merged = None
    for n in range(N_BRANCH):
        term = _sigmoid(gl[n]) * _dot(br_sc[n], wbr_ref[n])
        merged = term if merged is None else merged + term
    o_ref[0] = x + _dot(merged.astype(BF16), wout_ref[...])


def _ffn_kernel(final, x_ref, g2_ref, wup_ref, cw_ref, cb_ref, wdn_ref, gf_ref, o_ref, tail_sc):
    ts = x_ref.shape[1]
    dff = wdn_ref.shape[0]
    s_idx = pl.program_id(1)

    @pl.when(s_idx == 0)
    def _start_of_sequence():
        tail_sc[...] = jnp.zeros_like(tail_sc)

    x = x_ref[0]
    hb = (_rms(x) * g2_ref[...]).astype(BF16)
    a = _dot(hb, wup_ref[:, :dff])
    b = _dot(hb, wup_ref[:, dff:])
    row = lax.broadcasted_iota(jnp.int32, (ts, 1), 0)
    prev1 = tail_sc[7:8, :]
    prev2 = tail_sc[6:7, :]
    a1 = jnp.where(row == 0, prev1, pltpu.roll(a, 1, axis=0))
    a2 = jnp.where(row == 0, prev2, jnp.where(row == 1, prev1, pltpu.roll(a, 2, axis=0)))
    tail_sc[...] = a[ts - 8:, :]
    ac = cb_ref[...] + cw_ref[0:1, :] * a2 + cw_ref[1:2, :] * a1 + cw_ref[2:3, :] * a
    y = (_silu(ac) * b).astype(BF16)
    out = x + _dot(y, wdn_ref[...])
    if final:
        out = _rms(out) * gf_ref[...]
    o_ref[0] = out


def _const_spec(shape):
    return pl.BlockSpec(shape, lambda b, s: (0,) * len(shape), pipeline_mode=pl.Buffered(1))


def _mixer_call(layer, x, mem, g1, w_in, lng, lnb, wsp, bsp, lbl, hng, mg, wkv, wbr, wout):
    bsz, seq, d = x.shape
    mlen = mem.shape[1]
    ts = SEQ_TILE
    consts = (g1, w_in, lng, lnb, wsp, bsp, lbl, hng, mg, wkv, wbr, wout)
    return pl.pallas_call(
        functools.partial(_mixer_kernel, layer),
        grid=(bsz, seq // ts),
        in_specs=[pl.BlockSpec((1, ts, d), lambda b, s: (b, s, 0)),
                  pl.BlockSpec((1, mlen, d), lambda b, s: (b, 0, 0))]
                 + [_const_spec(c.shape) for c in consts],
        out_specs=pl.BlockSpec((1, ts, d), lambda b, s: (b, s, 0)),
        out_shape=jax.ShapeDtypeStruct(x.shape, F32),
        scratch_shapes=[pltpu.VMEM((XA_WIDTH, mlen), BF16),
                        pltpu.VMEM((mlen, XA_WIDTH), BF16),
                        pltpu.VMEM((HG_HEADS, HG_DIM, HG_DIM), F32),
                        pltpu.VMEM((N_BRANCH, ts, BRANCH_WIDTH), BF16),
                        pltpu.VMEM((ts, HG_WIDTH), F32)],
        compiler_params=pltpu.CompilerParams(
            dimension_semantics=("parallel", "arbitrary"),
            vmem_limit_bytes=V7X_VMEM_LIMIT_BYTES),
        name="mixer",
    )(x, mem, *consts)


def _ffn_call(final, x, g2, wup, cw, cb, wdn, gf):
    bsz, seq, d = x.shape
    dff = wdn.shape[0]
    ts = SEQ_TILE
    consts = (g2, wup, cw, cb, wdn, gf)
    return pl.pallas_call(
        functools.partial(_ffn_kernel, final),
        grid=(bsz, seq // ts),
        in_specs=[pl.BlockSpec((1, ts, d), lambda b, s: (b, s, 0))]
                 + [_const_spec(c.shape) for c in consts],
        out_specs=pl.BlockSpec((1, ts, d), lambda b, s: (b, s, 0)),
        out_shape=jax.ShapeDtypeStruct(x.shape, F32),
        scratch_shapes=[pltpu.VMEM((8, dff), F32)],
        compiler_params=pltpu.CompilerParams(
            dimension_semantics=("parallel", "arbitrary"),
            vmem_limit_bytes=V7X_VMEM_LIMIT_BYTES),
        name="ffn",
    )(x, *consts)


def kernel(x, mem, norm1_g, w_in, ln_v_g, ln_v_b, w_spatial, b_spatial, lb_logits, hgrn_norm_g,
           mem_norm_g, w_mem_kv, w_branch, w_out, norm2_g, w_up, conv_w, conv_b, w_down, final_g):
    depth = w_in.shape[0]
    assert x.shape[1] % SEQ_TILE == 0 and SEQ_TILE % GM_CHUNK == 0 and SEQ_TILE % HG_CHUNK == 0
    assert HG_CHUNK % HG_SUB == 0 and HG_SUB % 16 == 0
    row = lambda p: p.reshape(1, -1)
    gd = GM_WIDTH // GM_GROUPS
    for l in range(depth):
        bsp = jnp.repeat(b_spatial[l].T, gd, axis=1)
        x = _mixer_call(l, x, mem, row(norm1_g[l]), w_in[l].astype(BF16), row(ln_v_g[l]),
                        row(ln_v_b[l]), w_spatial[l].astype(BF16), bsp, lb_logits,
                        row(hgrn_norm_g[l]), row(mem_norm_g[l]), w_mem_kv[l].astype(BF16),
                        w_branch[l].astype(BF16), w_out[l].astype(BF16))
        x = _ffn_call(l == depth - 1, x, row(norm2_g[l]), w_up[l].astype(BF16), conv_w[l],
                      row(conv_b[l]), w_down[l].astype(BF16), row(final_g))
    return x
```

```python
import functools

import jax
import jax.numpy as jnp
from jax import lax
from jax.experimental import pallas as pl
from jax.experimental.pallas import tpu as pltpu

F32 = jnp.float32
BF16 = jnp.bfloat16

EPS = 1e-6
GM_GROUPS = 4
GM_CHUNK = 128
GM_WIDTH = 512
HG_HEADS = 4
HG_DIM = 128
HG_WIDTH = HG_HEADS * HG_DIM
HG_CHUNK = 64
HG_SUB = 32
XA_HEADS = 4
XA_DIM = 128
XA_WIDTH = XA_HEADS * XA_DIM
N_BRANCH = 3
BRANCH_WIDTH = 512

SEQ_TILE = 512
V7X_VMEM_LIMIT_BYTES = 56 * 1024 * 1024
V7X_MXU_DEPTH = 256


def _dot(a, b):
    return jnp.dot(a, b, preferred_element_type=F32)


def _dot_nt(a, b):
    return lax.dot_general(a, b, (((1,), (1,)), ((), ())), preferred_element_type=F32)


def _dot_tn(a, b):
    return lax.dot_general(a, b, (((0,), (0,)), ((), ())), preferred_element_type=F32)


def _silu_half(h):
    return h * jnp.tanh(h) + h


GELU_K0 = 2.0 * 0.7978845608028654
GELU_K1 = 8.0 * 0.044715 * 0.7978845608028654


def _gelu_half(h):
    return h * jnp.tanh(h * (GELU_K0 + GELU_K1 * (h * h))) + h


def _rms(x):
    return x * lax.rsqrt(jnp.mean(x * x, axis=-1, keepdims=True) + EPS)


def _mixer_kernel(layer, x_ref, mem_ref, g1_ref, w_in_ref, lng_ref, lnb_ref, wsp_ref, bsp_ref,
                  lbl_ref, hng_ref, mg_ref, wkv_ref, wbr_ref, wout_ref, o_ref,
                  kt_sc, v_sc, st_sc, br_sc, ho_sc):
    ts = x_ref.shape[1]
    s_idx = pl.program_id(1)

    @pl.when(s_idx == 0)
    def _start_of_sequence():
        m = mem_ref[0]
        mn = (_rms(m) * mg_ref[...]).astype(BF16)
        kv = _dot(mn, wkv_ref[...])
        kt_sc[...] = kv[:, :XA_WIDTH].T.astype(BF16)
        v_sc[...] = kv[:, XA_WIDTH:].astype(BF16)
        st_sc[...] = jnp.zeros_like(st_sc)

    x = x_ref[0]
    hb = (_rms(x) * g1_ref[...]).astype(BF16)
    d = x.shape[-1]

    def proj(lo, width):
        return _dot(hb, w_in_ref[:, lo:lo + width])

    c_u, c_v = 0, GM_WIDTH
    c_q = 2 * GM_WIDTH
    c_f, c_i, c_g = c_q + HG_WIDTH, c_q + 2 * HG_WIDTH, c_q + 3 * HG_WIDTH
    c_x = c_q + 4 * HG_WIDTH
    c_gate = c_x + XA_WIDTH

    p_u = proj(c_u, GM_WIDTH)
    p_v = proj(c_v, GM_WIDTH)
    hq = proj(c_q, HG_WIDTH)
    hf = proj(c_f, HG_WIDTH)
    hi = proj(c_i, HG_WIDTH)
    hg = proj(c_g, HG_WIDTH)

    u = _gelu_half(p_u)
    v = _gelu_half(p_v)
    mu = jnp.mean(v, axis=-1, keepdims=True)
    vc = v - mu
    var = jnp.mean(vc * vc, axis=-1, keepdims=True)
    vn = ((vc * lax.rsqrt(var + EPS)) * lng_ref[...] + lnb_ref[...]).astype(BF16)
    p_x = proj(c_x, XA_WIDTH)
    r128 = lax.broadcasted_iota(jnp.int32, (GM_CHUNK, GM_CHUNK), 0)
    c128 = lax.broadcasted_iota(jnp.int32, (GM_CHUNK, GM_CHUNK), 1)
    tril128 = c128 <= r128
    gd = GM_WIDTH // GM_GROUPS
    for g in range(GM_GROUPS):
        w_g = jnp.where(tril128, wsp_ref[g], jnp.zeros((), BF16))
        for c in range(ts // GM_CHUNK):
            rows = slice(c * GM_CHUNK, (c + 1) * GM_CHUNK)
            cols = slice(g * gd, (g + 1) * gd)
            mixed = _dot(w_g, vn[rows, cols]) + bsp_ref[:, cols]
            br_sc[0, rows, cols] = (u[rows, cols] * mixed).astype(BF16)
    gl = [proj(c_gate, d)]

    ll = lbl_ref[...]
    le = jnp.exp(ll - jnp.max(ll, axis=0, keepdims=True))
    lb = jnp.sum(le[:layer + 1], axis=0, keepdims=True) / jnp.sum(le, axis=0, keepdims=True)
    fg = (0.5 + 0.5 * lb) + (0.5 - 0.5 * lb) * jnp.tanh(hf)
    kk = 1.0 - fg
    logf = jnp.log(fg)
    qf = _silu_half(hq)
    cr = min(ts, V7X_MXU_DEPTH)
    rt = lax.broadcasted_iota(jnp.int32, (cr, cr), 0)
    ct = lax.broadcasted_iota(jnp.int32, (cr, cr), 1)
    cum_m = jnp.where((ct <= rt) & (rt // HG_SUB == ct // HG_SUB), 1.0, 0.0).astype(BF16)
    lf_hi = logf.astype(BF16)
    lf_lo = (logf - lf_hi.astype(F32)).astype(BF16)
    a_sub = jnp.concatenate(
        [_dot(cum_m, lf_hi[r:r + cr]) + _dot(cum_m, lf_lo[r:r + cr]) for r in range(0, ts, cr)],
        axis=0)

    xq = (p_x * (XA_DIM ** -0.5)).astype(BF16)
    att_s = [_dot(xq[:, h * XA_DIM:(h + 1) * XA_DIM], kt_sc[h * XA_DIM:(h + 1) * XA_DIM, :])
             for h in range(XA_HEADS)]
    gl.append(proj(c_gate + d, d))

    rc = lax.broadcasted_iota(jnp.int32, (HG_CHUNK, HG_CHUNK), 0)
    cc = lax.broadcasted_iota(jnp.int32, (HG_CHUNK, HG_CHUNK), 1)
    mask_same = (cc <= rc) & (rc // HG_SUB == cc // HG_SUB)
    mask_later = rc // HG_SUB > cc // HG_SUB
    vb = hi.astype(BF16)
    nsub = HG_CHUNK // HG_SUB
    chunks = []
    for c in range(ts // HG_CHUNK):
        a_blk = [a_sub[c * HG_CHUNK + j * HG_SUB:c * HG_CHUNK + (j + 1) * HG_SUB] for j in range(nsub)]
        tot = [a_j[HG_SUB - 1:HG_SUB] for a_j in a_blk]
        qd, kd, q1, k1, qi, ke = [], [], [], [], [], []
        for j in range(nsub):
            r0 = c * HG_CHUNK + j * HG_SUB
            q_j, k_j, a_j = qf[r0:r0 + HG_SUB], kk[r0:r0 + HG_SUB], a_blk[j]
            z = a_j - a_j[HG_SUB // 2 - 1:HG_SUB // 2]
            qd.append(q_j * jnp.exp(z))
            kd.append(k_j * jnp.exp(-z))
            q1_j = q_j * jnp.exp(a_j)
            k1_j = k_j * jnp.exp(tot[j] - a_j)
            q1.append(q1_j)
            k1.append(k1_j)
            before = sum(tot[:j]) if j else None
            after = sum(tot[j + 1:]) if j + 1 < nsub else None
            qi.append(q1_j if before is None else q1_j * jnp.exp(before))
            ke.append(k1_j if after is None else k1_j * jnp.exp(after))
        cat = lambda parts: jnp.concatenate(parts, axis=0).astype(BF16)
        chunks.append((cat(qd), cat(kd), cat(q1), cat(k1), cat(qi), cat(ke), jnp.exp(sum(tot))))
    pre = {}
    for c, (qd, kd, q1, k1, qi, ke, dec) in enumerate(chunks):
        rows = slice(c * HG_CHUNK, (c + 1) * HG_CHUNK)
        for h in range(HG_HEADS):
            cols = slice(h * HG_DIM, (h + 1) * HG_DIM)
            pre[c, h] = (_dot_nt(qd[:, cols], kd[:, cols]), _dot_nt(q1[:, cols], k1[:, cols]),
                         _dot_tn(vb[rows, cols], ke[:, cols]))

    for h in range(XA_HEADS):
        cols = slice(h * XA_DIM, (h + 1) * XA_DIM)
        s = att_s[h]
        e = jnp.exp(s - jnp.max(s, axis=-1, keepdims=True))
        inv = 1.0 / jnp.sum(e, axis=-1, keepdims=True)
        br_sc[2, :, cols] = (_dot(e.astype(BF16), v_sc[:, cols]) * inv).astype(BF16)
    gl.append(proj(c_gate + 2 * d, d))

    st = [st_sc[h] for h in range(HG_HEADS)]
    for c, (qd, kd, q1, k1, qi, ke, dec) in enumerate(chunks):
        rows = slice(c * HG_CHUNK, (c + 1) * HG_CHUNK)
        for h in range(HG_HEADS):
            cols = slice(h * HG_DIM, (h + 1) * HG_DIM)
            s_same, s_later, upd = pre[c, h]
            sc = jnp.where(mask_same, s_same, jnp.where(mask_later, s_later, 0.0))
            o = _dot(sc.astype(BF16), vb[rows, cols]) + _dot_nt(qi[:, cols], st[h].astype(BF16))
            ho_sc[rows, cols] = o
            st[h] = st[h] * dec[:, cols] + upd
    for h in range(HG_HEADS):
        st_sc[h] = st[h]
    hg_act = _silu_half(hg)
    for h in range(HG_HEADS):
        cols = slice(h * HG_DIM, (h + 1) * HG_DIM)
        o = ho_sc[:, cols]
        on = _rms(o) * hng_ref[...]
        br_sc[1, :, cols] = (on * hg_act[:, cols]).astype(BF16)

    merged = None
    for n in range(N_BRANCH):
        up = _dot(br_sc[n], wbr_ref[n])
        term = jnp.tanh(gl[n]) * up + up
        merged = term if merged is None else merged + term
    o_ref[0] = x + _dot(merged.astype(BF16), wout_ref[...])


def _ffn_kernel(final, x_ref, g2_ref, wup_ref, cw_ref, cb_ref, wdn_ref, gf_ref, o_ref, tail_sc):
    ts = x_ref.shape[1]
    dff = wdn_ref.shape[0]
    s_idx = pl.program_id(1)

    @pl.when(s_idx == 0)
    def _start_of_sequence():
        tail_sc[...] = jnp.zeros_like(tail_sc)

    x = x_ref[0]
    hb = (_rms(x) * g2_ref[...]).astype(BF16)
    a = _dot(hb, wup_ref[:, :dff])
    b = _dot(hb, wup_ref[:, dff:])
    row = lax.broadcasted_iota(jnp.int32, (ts, 1), 0)
    prev1 = tail_sc[7:8, :]
    prev2 = tail_sc[6:7, :]
    a1 = jnp.where(row == 0, prev1, pltpu.roll(a, 1, axis=0))
    a2 = jnp.where(row == 0, prev2, jnp.where(row == 1, prev1, pltpu.roll(a, 2, axis=0)))
    tail_sc[...] = a[ts - 8:, :]
    ac = cb_ref[...] + cw_ref[0:1, :] * a2 + cw_ref[1:2, :] * a1 + cw_ref[2:3, :] * a
    y = (_silu_half(ac) * b).astype(BF16)
    out = x + _dot(y, wdn_ref[...])
    if final:
        out = _rms(out) * gf_ref[...]
    o_ref[0] = out


def _const_spec(shape):
    return pl.BlockSpec(shape, lambda b, s: (0,) * len(shape), pipeline_mode=pl.Buffered(1))


def _mixer_call(layer, x, mem, g1, w_in, lng, lnb, wsp, bsp, lbl, hng, mg, wkv, wbr, wout):
    bsz, seq, d = x.shape
    mlen = mem.shape[1]
    ts = SEQ_TILE
    consts = (g1, w_in, lng, lnb, wsp, bsp, lbl, hng, mg, wkv, wbr, wout)
    return pl.pallas_call(
        functools.partial(_mixer_kernel, layer),
        grid=(bsz, seq // ts),
        in_specs=[pl.BlockSpec((1, ts, d), lambda b, s: (b, s, 0)),
                  pl.BlockSpec((1, mlen, d), lambda b, s: (b, 0, 0))]
                 + [_const_spec(c.shape) for c in consts],
        out_specs=pl.BlockSpec((1, ts, d), lambda b, s: (b, s, 0)),
        out_shape=jax.ShapeDtypeStruct(x.shape, F32),
        scratch_shapes=[pltpu.VMEM((XA_WIDTH, mlen), BF16),
                        pltpu.VMEM((mlen, XA_WIDTH), BF16),
                        pltpu.VMEM((HG_HEADS, HG_DIM, HG_DIM), F32),
                        pltpu.VMEM((N_BRANCH, ts, BRANCH_WIDTH), BF16),
                        pltpu.VMEM((ts, HG_WIDTH), F32)],
        compiler_params=pltpu.CompilerParams(
            dimension_semantics=("parallel", "arbitrary"),
            vmem_limit_bytes=V7X_VMEM_LIMIT_BYTES),
        name="mixer",
    )(x, mem, *consts)


def _ffn_call(final, x, g2, wup, cw, cb, wdn, gf):
    bsz, seq, d = x.shape
    dff = wdn.shape[0]
    ts = SEQ_TILE
    consts = (g2, wup, cw, cb, wdn, gf)
    return pl.pallas_call(
        functools.partial(_ffn_kernel, final),
        grid=(bsz, seq // ts),
        in_specs=[pl.BlockSpec((1, ts, d), lambda b, s: (b, s, 0))]
                 + [_const_spec(c.shape) for c in consts],
        out_specs=pl.BlockSpec((1, ts, d), lambda b, s: (b, s, 0)),
        out_shape=jax.ShapeDtypeStruct(x.shape, F32),
        scratch_shapes=[pltpu.VMEM((8, dff), F32)],
        compiler_params=pltpu.CompilerParams(
            dimension_semantics=("parallel", "arbitrary"),
            vmem_limit_bytes=V7X_VMEM_LIMIT_BYTES),
        name="ffn",
    )(x, *consts)


def kernel(x, mem, norm1_g, w_in, ln_v_g, ln_v_b, w_spatial, b_spatial, lb_logits, hgrn_norm_g,
           mem_norm_g, w_mem_kv, w_branch, w_out, norm2_g, w_up, conv_w, conv_b, w_down, final_g):
    depth = w_in.shape[0]
    assert x.shape[1] % SEQ_TILE == 0 and SEQ_TILE % GM_CHUNK == 0 and SEQ_TILE % HG_CHUNK == 0
    assert HG_CHUNK % HG_SUB == 0 and HG_SUB % 16 == 0
    row = lambda p: p.reshape(1, -1)
    gd = GM_WIDTH // GM_GROUPS
    d = x.shape[-1]
    sections = ((GM_WIDTH, 0.5), (GM_WIDTH, 0.5), (HG_WIDTH, 0.5), (HG_WIDTH, 0.5), (HG_WIDTH, 1.0),
                (HG_WIDTH, 0.5), (XA_WIDTH, 1.0), (N_BRANCH * d, 0.5))
    in_scale = jnp.concatenate([jnp.full((1, w), s, F32) for w, s in sections], axis=1)
    for l in range(depth):
        bsp = jnp.repeat(b_spatial[l].T, gd, axis=1)
        x = _mixer_call(l, x, mem, row(norm1_g[l]), (w_in[l] * in_scale).astype(BF16),
                        row(ln_v_g[l]), row(ln_v_b[l]), w_spatial[l].astype(BF16), bsp, lb_logits,
                        row(hgrn_norm_g[l]), row(mem_norm_g[l]), w_mem_kv[l].astype(BF16),
                        (0.5 * w_branch[l]).astype(BF16), w_out[l].astype(BF16))
        x = _ffn_call(l == depth - 1, x, row(norm2_g[l]), w_up[l].astype(BF16), 0.5 * conv_w[l],
                      row(0.5 * conv_b[l]), w_down[l].astype(BF16), row(final_g))
    return x
```

```python
import functools

import jax
import jax.numpy as jnp
from jax import lax
from jax.experimental import pallas as pl
from jax.experimental.pallas import tpu as pltpu

F32 = jnp.float32
BF16 = jnp.bfloat16

EPS = 1e-6
GM_GROUPS = 4
GM_CHUNK = 128
GM_WIDTH = 512
HG_HEADS = 4
HG_DIM = 128
HG_WIDTH = HG_HEADS * HG_DIM
HG_CHUNK = 64
HG_SUB = 32
XA_HEADS = 4
XA_DIM = 128
XA_WIDTH = XA_HEADS * XA_DIM
N_BRANCH = 3
BRANCH_WIDTH = 512

SEQ_TILE = 512
V7X_VMEM_LIMIT_BYTES = 56 * 1024 * 1024
V7X_MXU_DEPTH = 256
BF16_ROWS = 16


def _dot(a, b):
    return jnp.dot(a, b, preferred_element_type=F32)


def _dot_nt(a, b):
    return lax.dot_general(a, b, (((1,), (1,)), ((), ())), preferred_element_type=F32)


def _dot_tn(a, b):
    return lax.dot_general(a, b, (((0,), (0,)), ((), ())), preferred_element_type=F32)


def _silu_half(h):
    return h * jnp.tanh(h) + h


GELU_K0 = 2.0 * 0.7978845608028654
GELU_K1 = 8.0 * 0.044715 * 0.7978845608028654


def _gelu_half(h):
    return h * jnp.tanh(h * (GELU_K0 + GELU_K1 * (h * h))) + h


def _rms(x):
    return x * lax.rsqrt(jnp.mean(x * x, axis=-1, keepdims=True) + EPS)


def _mixer_kernel(layer, x_ref, mem_ref, g1_ref, w_in_ref, lng_ref, lnb_ref, wsp_ref, bsp_ref,
                  lbl_ref, hng_ref, mg_ref, wkv_ref, wbr_ref, wout_ref, wup_f32_ref, wdn_f32_ref,
                  o_ref, wup_bf16_ref, wdn_bf16_ref, kt_sc, v_sc, st_sc, br_sc, ho_sc):
    ts = x_ref.shape[1]
    s_idx = pl.program_id(1)

    wup_bf16_ref[...] = wup_f32_ref[...].astype(BF16)
    wdn_bf16_ref[...] = wdn_f32_ref[...].astype(BF16)

    @pl.when(s_idx == 0)
    def _start_of_sequence():
        m = mem_ref[0]
        mn = (_rms(m) * mg_ref[...]).astype(BF16)
        kv = _dot(mn, wkv_ref[...])
        kt_sc[...] = kv[:, :XA_WIDTH].T.astype(BF16)
        v_sc[...] = kv[:, XA_WIDTH:].astype(BF16)
        st_sc[...] = jnp.zeros_like(st_sc)

    x = x_ref[0]
    hb = (_rms(x) * g1_ref[...]).astype(BF16)
    d = x.shape[-1]

    def proj(lo, width):
        return _dot(hb, w_in_ref[:, lo:lo + width])

    c_u, c_v = 0, GM_WIDTH
    c_q = 2 * GM_WIDTH
    c_f, c_i, c_g = c_q + HG_WIDTH, c_q + 2 * HG_WIDTH, c_q + 3 * HG_WIDTH
    c_x = c_q + 4 * HG_WIDTH
    c_gate = c_x + XA_WIDTH

    p_u = proj(c_u, GM_WIDTH)
    p_v = proj(c_v, GM_WIDTH)
    hq = proj(c_q, HG_WIDTH)
    hf = proj(c_f, HG_WIDTH)
    hi = proj(c_i, HG_WIDTH)
    hg = proj(c_g, HG_WIDTH)

    u = _gelu_half(p_u)
    v = _gelu_half(p_v)
    mu = jnp.mean(v, axis=-1, keepdims=True)
    vc = v - mu
    var = jnp.mean(vc * vc, axis=-1, keepdims=True)
    vn = ((vc * lax.rsqrt(var + EPS)) * lng_ref[...] + lnb_ref[...]).astype(BF16)
    p_x = proj(c_x, XA_WIDTH)
    r128 = lax.broadcasted_iota(jnp.int32, (GM_CHUNK, GM_CHUNK), 0)
    c128 = lax.broadcasted_iota(jnp.int32, (GM_CHUNK, GM_CHUNK), 1)
    tril128 = c128 <= r128
    gd = GM_WIDTH // GM_GROUPS
    for g in range(GM_GROUPS):
        w_g = jnp.where(tril128, wsp_ref[g], jnp.zeros((), BF16))
        for c in range(ts // GM_CHUNK):
            rows = slice(c * GM_CHUNK, (c + 1) * GM_CHUNK)
            cols = slice(g * gd, (g + 1) * gd)
            mixed = _dot(w_g, vn[rows, cols]) + bsp_ref[:, cols]
            br_sc[0, rows, cols] = (u[rows, cols] * mixed).astype(BF16)
    gl = [proj(c_gate, d)]

    ll = lbl_ref[...]
    le = jnp.exp(ll - jnp.max(ll, axis=0, keepdims=True))
    lb = jnp.sum(le[:layer + 1], axis=0, keepdims=True) / jnp.sum(le, axis=0, keepdims=True)
    fg = (0.5 + 0.5 * lb) + (0.5 - 0.5 * lb) * jnp.tanh(hf)
    kk = 1.0 - fg
    logf = jnp.log(fg)
    qf = _silu_half(hq)
    cr = min(ts, V7X_MXU_DEPTH)
    rt = lax.broadcasted_iota(jnp.int32, (cr, cr), 0)
    ct = lax.broadcasted_iota(jnp.int32, (cr, cr), 1)
    cum_m = jnp.where((ct <= rt) & (rt // HG_SUB == ct // HG_SUB), 1.0, 0.0).astype(BF16)
    lf_hi = logf.astype(BF16)
    lf_lo = (logf - lf_hi.astype(F32)).astype(BF16)
    a_sub = jnp.concatenate(
        [_dot(cum_m, lf_hi[r:r + cr]) + _dot(cum_m, lf_lo[r:r + cr]) for r in range(0, ts, cr)],
        axis=0)

    xq = (p_x * (XA_DIM ** -0.5)).astype(BF16)
    att_s = [_dot(xq[:, h * XA_DIM:(h + 1) * XA_DIM], kt_sc[h * XA_DIM:(h + 1) * XA_DIM, :])
             for h in range(XA_HEADS)]
    gl.append(proj(c_gate + d, d))

    rc = lax.broadcasted_iota(jnp.int32, (HG_CHUNK, HG_CHUNK), 0)
    cc = lax.broadcasted_iota(jnp.int32, (HG_CHUNK, HG_CHUNK), 1)
    mask_same = (cc <= rc) & (rc // HG_SUB == cc // HG_SUB)
    mask_later = rc // HG_SUB > cc // HG_SUB
    vb = hi.astype(BF16)
    nsub = HG_CHUNK // HG_SUB
    chunks = []
    for c in range(ts // HG_CHUNK):
        a_blk = [a_sub[c * HG_CHUNK + j * HG_SUB:c * HG_CHUNK + (j + 1) * HG_SUB] for j in range(nsub)]
        tot = [a_j[HG_SUB - 1:HG_SUB] for a_j in a_blk]
        qd, kd, q1, k1, qi, ke = [], [], [], [], [], []
        for j in range(nsub):
            r0 = c * HG_CHUNK + j * HG_SUB
            q_j, k_j, a_j = qf[r0:r0 + HG_SUB], kk[r0:r0 + HG_SUB], a_blk[j]
            z = a_j - a_j[HG_SUB // 2 - 1:HG_SUB // 2]
            qd.append(q_j * jnp.exp(z))
            kd.append(k_j * jnp.exp(-z))
            q1_j = q_j * jnp.exp(a_j)
            k1_j = k_j * jnp.exp(tot[j] - a_j)
            q1.append(q1_j)
            k1.append(k1_j)
            before = sum(tot[:j]) if j else None
            after = sum(tot[j + 1:]) if j + 1 < nsub else None
            qi.append(q1_j if before is None else q1_j * jnp.exp(before))
            ke.append(k1_j if after is None else k1_j * jnp.exp(after))
        cat = lambda parts: jnp.concatenate(parts, axis=0).astype(BF16)
        chunks.append((cat(qd), cat(kd), cat(q1), cat(k1), cat(qi), cat(ke), jnp.exp(sum(tot))))
    pre = {}
    for c, (qd, kd, q1, k1, qi, ke, dec) in enumerate(chunks):
        rows = slice(c * HG_CHUNK, (c + 1) * HG_CHUNK)
        for h in range(HG_HEADS):
            cols = slice(h * HG_DIM, (h + 1) * HG_DIM)
            pre[c, h] = (_dot_nt(qd[:, cols], kd[:, cols]), _dot_nt(q1[:, cols], k1[:, cols]),
                         _dot_tn(vb[rows, cols], ke[:, cols]))

    for h in range(XA_HEADS):
        cols = slice(h * XA_DIM, (h + 1) * XA_DIM)
        s = att_s[h]
        e = jnp.exp(s - jnp.max(s, axis=-1, keepdims=True))
        inv = 1.0 / jnp.sum(e, axis=-1, keepdims=True)
        br_sc[2, :, cols] = (_dot(e.astype(BF16), v_sc[:, cols]) * inv).astype(BF16)
    gl.append(proj(c_gate + 2 * d, d))

    st = [st_sc[h] for h in range(HG_HEADS)]
    for c, (qd, kd, q1, k1, qi, ke, dec) in enumerate(chunks):
        rows = slice(c * HG_CHUNK, (c + 1) * HG_CHUNK)
        for h in range(HG_HEADS):
            cols = slice(h * HG_DIM, (h + 1) * HG_DIM)
            s_same, s_later, upd = pre[c, h]
            sc = jnp.where(mask_same, s_same, jnp.where(mask_later, s_later, 0.0))
            o = _dot(sc.astype(BF16), vb[rows, cols]) + _dot_nt(qi[:, cols], st[h].astype(BF16))
            ho_sc[rows, cols] = o
            st[h] = st[h] * dec[:, cols] + upd
    for h in range(HG_HEADS):
        st_sc[h] = st[h]
    hg_act = _silu_half(hg)
    for h in range(HG_HEADS):
        cols = slice(h * HG_DIM, (h + 1) * HG_DIM)
        o = ho_sc[:, cols]
        on = _rms(o) * hng_ref[...]
        br_sc[1, :, cols] = (on * hg_act[:, cols]).astype(BF16)

    merged = None
    for n in range(N_BRANCH):
        up = _dot(br_sc[n], wbr_ref[n])
        term = jnp.tanh(gl[n]) * up + up
        merged = term if merged is None else merged + term
    o_ref[0] = x + _dot(merged.astype(BF16), wout_ref[...])


def _ffn_kernel(final, x_ref, g2_ref, wup_ref, cw_ref, cb_ref, wdn_ref, gf_ref, o_ref, tail_sc):
    ts = x_ref.shape[1]
    dff = wdn_ref.shape[0]
    s_idx = pl.program_id(1)

    @pl.when(s_idx == 0)
    def _start_of_sequence():
        tail_sc[...] = jnp.zeros_like(tail_sc)

    x = x_ref[0]
    hb = (_rms(x) * g2_ref[...]).astype(BF16)
    a = _dot(hb, wup_ref[:, :dff])
    b = _dot(hb, wup_ref[:, dff:])
    row = lax.broadcasted_iota(jnp.int32, (ts, 1), 0)
    prev1 = tail_sc[7:8, :]
    prev2 = tail_sc[6:7, :]
    a1 = jnp.where(row == 0, prev1, pltpu.roll(a, 1, axis=0))
    a2 = jnp.where(row == 0, prev2, jnp.where(row == 1, prev1, pltpu.roll(a, 2, axis=0)))
    tail_sc[...] = a[ts - 8:, :]
    ac = cb_ref[...] + cw_ref[0:1, :] * a2 + cw_ref[1:2, :] * a1 + cw_ref[2:3, :] * a
    y = (_silu_half(ac) * b).astype(BF16)
    out = x + _dot(y, wdn_ref[...])
    if final:
        out = _rms(out) * gf_ref[...]
    o_ref[0] = out


def _const_spec(shape):
    return pl.BlockSpec(shape, lambda b, s: (0,) * len(shape), pipeline_mode=pl.Buffered(1))


def _cast_block_spec(n_rows, n_cols, n_seq, n_steps):
    rows = next(r for r in range(BF16_ROWS, n_rows + 1, BF16_ROWS)
                if n_rows % r == 0 and n_rows // r <= n_steps)
    last = n_rows // rows - 1
    return pl.BlockSpec((rows, n_cols), lambda b, s: (jnp.minimum(b * n_seq + s, last), 0))


def _mixer_call(layer, x, mem, g1, w_in, lng, lnb, wsp, bsp, lbl, hng, mg, wkv, wbr, wout,
                wup_f32, wdn_f32):
    bsz, seq, d = x.shape
    mlen = mem.shape[1]
    ts = SEQ_TILE
    n_seq = seq // ts
    consts = (g1, w_in, lng, lnb, wsp, bsp, lbl, hng, mg, wkv, wbr, wout)
    cast_specs = [_cast_block_spec(*w.shape, n_seq, bsz * n_seq) for w in (wup_f32, wdn_f32)]
    return pl.pallas_call(
        functools.partial(_mixer_kernel, layer),
        grid=(bsz, n_seq),
        in_specs=[pl.BlockSpec((1, ts, d), lambda b, s: (b, s, 0)),
                  pl.BlockSpec((1, mlen, d), lambda b, s: (b, 0, 0))]
                 + [_const_spec(c.shape) for c in consts] + cast_specs,
        out_specs=[pl.BlockSpec((1, ts, d), lambda b, s: (b, s, 0))] + cast_specs,
        out_shape=[jax.ShapeDtypeStruct(x.shape, F32),
                   jax.ShapeDtypeStruct(wup_f32.shape, BF16),
                   jax.ShapeDtypeStruct(wdn_f32.shape, BF16)],
        scratch_shapes=[pltpu.VMEM((XA_WIDTH, mlen), BF16),
                        pltpu.VMEM((mlen, XA_WIDTH), BF16),
                        pltpu.VMEM((HG_HEADS, HG_DIM, HG_DIM), F32),
                        pltpu.VMEM((N_BRANCH, ts, BRANCH_WIDTH), BF16),
                        pltpu.VMEM((ts, HG_WIDTH), F32)],
        compiler_params=pltpu.CompilerParams(
            dimension_semantics=("arbitrary", "arbitrary"),
            vmem_limit_bytes=V7X_VMEM_LIMIT_BYTES),
        name="mixer",
    )(x, mem, *consts, wup_f32, wdn_f32)


def _ffn_call(final, x, g2, wup, cw, cb, wdn, gf):
    bsz, seq, d = x.shape
    dff = wdn.shape[0]
    ts = SEQ_TILE
    consts = (g2, wup, cw, cb, wdn, gf)
    return pl.pallas_call(
        functools.partial(_ffn_kernel, final),
        grid=(bsz, seq // ts),
        in_specs=[pl.BlockSpec((1, ts, d), lambda b, s: (b, s, 0))]
                 + [_const_spec(c.shape) for c in consts],
        out_specs=pl.BlockSpec((1, ts, d), lambda b, s: (b, s, 0)),
        out_shape=jax.ShapeDtypeStruct(x.shape, F32),
        scratch_shapes=[pltpu.VMEM((8, dff), F32)],
        compiler_params=pltpu.CompilerParams(
            dimension_semantics=("parallel", "arbitrary"),
            vmem_limit_bytes=V7X_VMEM_LIMIT_BYTES),
        name="ffn",
    )(x, *consts)


def kernel(x, mem, norm1_g, w_in, ln_v_g, ln_v_b, w_spatial, b_spatial, lb_logits, hgrn_norm_g,
           mem_norm_g, w_mem_kv, w_branch, w_out, norm2_g, w_up, conv_w, conv_b, w_down, final_g):
    depth = w_in.shape[0]
    assert x.shape[1] % SEQ_TILE == 0 and SEQ_TILE % GM_CHUNK == 0 and SEQ_TILE % HG_CHUNK == 0
    assert HG_CHUNK % HG_SUB == 0 and HG_SUB % 16 == 0
    row = lambda p: p.reshape(1, -1)
    gd = GM_WIDTH // GM_GROUPS
    d = x.shape[-1]
    sections = ((GM_WIDTH, 0.5), (GM_WIDTH, 0.5), (HG_WIDTH, 0.5), (HG_WIDTH, 0.5), (HG_WIDTH, 1.0),
                (HG_WIDTH, 0.5), (XA_WIDTH, 1.0), (N_BRANCH * d, 0.5))
    in_scale = jnp.concatenate([jnp.full((1, w), s, F32) for w, s in sections], axis=1)
    for l in range(depth):
        bsp = jnp.repeat(b_spatial[l].T, gd, axis=1)
        x, wup, wdn = _mixer_call(
            l, x, mem, row(norm1_g[l]), (w_in[l] * in_scale).astype(BF16),
            row(ln_v_g[l]), row(ln_v_b[l]), w_spatial[l].astype(BF16), bsp, lb_logits,
            row(hgrn_norm_g[l]), row(mem_norm_g[l]), w_mem_kv[l].astype(BF16),
            (0.5 * w_branch[l]).astype(BF16), w_out[l].astype(BF16), w_up[l], w_down[l])
        x = _ffn_call(l == depth - 1, x, row(norm2_g[l]), wup, 0.5 * conv_w[l],
                      row(0.5 * conv_b[l]), wdn, row(final_g))
    return x
```

```python
import functools

import jax
import jax.numpy as jnp
from jax import lax
from jax.experimental import pallas as pl
from jax.experimental.pallas import tpu as pltpu

F32 = jnp.float32
BF16 = jnp.bfloat16

EPS = 1e-6
GM_GROUPS = 4
GM_CHUNK = 128
GM_WIDTH = 512
HG_HEADS = 4
HG_DIM = 128
HG_WIDTH = HG_HEADS * HG_DIM
HG_CHUNK = 64
HG_SUB = 32
XA_HEADS = 4
XA_DIM = 128
XA_WIDTH = XA_HEADS * XA_DIM
N_BRANCH = 3
BRANCH_WIDTH = 512

ROW_GROUP = 256
SEQ_TILE = 1024
V7X_VMEM_LIMIT_BYTES = 56 * 1024 * 1024
V7X_MXU_DEPTH = 256
BF16_ROWS = 16


def _dot(a, b):
    return jnp.dot(a, b, preferred_element_type=F32)


def _dot_nt(a, b):
    return lax.dot_general(a, b, (((1,), (1,)), ((), ())), preferred_element_type=F32)


def _dot_tn(a, b):
    return lax.dot_general(a, b, (((0,), (0,)), ((), ())), preferred_element_type=F32)


def _silu_half(h):
    return h * jnp.tanh(h) + h


GELU_K0 = 2.0 * 0.7978845608028654
GELU_K1 = 8.0 * 0.044715 * 0.7978845608028654


def _gelu_half(h):
    return h * jnp.tanh(h * (GELU_K0 + GELU_K1 * (h * h))) + h


def _rms(x):
    return x * lax.rsqrt(jnp.mean(x * x, axis=-1, keepdims=True) + EPS)


def _mixer_kernel(layer, x_ref, mem_ref, g1_ref, w_in_ref, lng_ref, lnb_ref, wsp_ref, bsp_ref,
                  lbl_ref, hng_ref, mg_ref, wkv_ref, wbr_ref, wout_ref, wup_f32_ref, wdn_f32_ref,
                  o_ref, wup_bf16_ref, wdn_bf16_ref, kt_sc, v_sc, st_sc, br_sc, ho_sc):
    ts = x_ref.shape[1]
    s_idx = pl.program_id(1)

    wup_bf16_ref[...] = wup_f32_ref[...].astype(BF16)
    wdn_bf16_ref[...] = wdn_f32_ref[...].astype(BF16)

    @pl.when(s_idx == 0)
    def _start_of_sequence():
        m = mem_ref[0]
        mn = (_rms(m) * mg_ref[...]).astype(BF16)
        kv = _dot(mn, wkv_ref[...])
        kt_sc[...] = kv[:, :XA_WIDTH].T.astype(BF16)
        v_sc[...] = kv[:, XA_WIDTH:].astype(BF16)
        st_sc[...] = jnp.zeros_like(st_sc)

    st = [st_sc[h] for h in range(HG_HEADS)]
    for r0 in range(0, ts, ROW_GROUP):
        st = _mixer_rows(layer, r0, st, x_ref, g1_ref, w_in_ref, lng_ref, lnb_ref, wsp_ref, bsp_ref,
                         lbl_ref, hng_ref, wbr_ref, wout_ref, o_ref, kt_sc, v_sc, br_sc, ho_sc)
    for h in range(HG_HEADS):
        st_sc[h] = st[h]


def _mixer_rows(layer, r0, st, x_ref, g1_ref, w_in_ref, lng_ref, lnb_ref, wsp_ref, bsp_ref,
                lbl_ref, hng_ref, wbr_ref, wout_ref, o_ref, kt_sc, v_sc, br_sc, ho_sc):
    ts = ROW_GROUP
    grp = slice(r0, r0 + ts)
    x = x_ref[0, grp]
    hb = (_rms(x) * g1_ref[...]).astype(BF16)
    d = x.shape[-1]

    def proj(lo, width):
        return _dot(hb, w_in_ref[:, lo:lo + width])

    c_u, c_v = 0, GM_WIDTH
    c_q = 2 * GM_WIDTH
    c_f, c_i, c_g = c_q + HG_WIDTH, c_q + 2 * HG_WIDTH, c_q + 3 * HG_WIDTH
    c_x = c_q + 4 * HG_WIDTH
    c_gate = c_x + XA_WIDTH

    p_u = proj(c_u, GM_WIDTH)
    p_v = proj(c_v, GM_WIDTH)
    hq = proj(c_q, HG_WIDTH)
    hf = proj(c_f, HG_WIDTH)
    hi = proj(c_i, HG_WIDTH)
    hg = proj(c_g, HG_WIDTH)

    u = _gelu_half(p_u)
    v = _gelu_half(p_v)
    mu = jnp.mean(v, axis=-1, keepdims=True)
    vc = v - mu
    var = jnp.mean(vc * vc, axis=-1, keepdims=True)
    vn = ((vc * lax.rsqrt(var + EPS)) * lng_ref[...] + lnb_ref[...]).astype(BF16)
    p_x = proj(c_x, XA_WIDTH)
    r128 = lax.broadcasted_iota(jnp.int32, (GM_CHUNK, GM_CHUNK), 0)
    c128 = lax.broadcasted_iota(jnp.int32, (GM_CHUNK, GM_CHUNK), 1)
    tril128 = c128 <= r128
    gd = GM_WIDTH // GM_GROUPS
    for g in range(GM_GROUPS):
        w_g = jnp.where(tril128, wsp_ref[g], jnp.zeros((), BF16))
        for c in range(ts // GM_CHUNK):
            rows = slice(c * GM_CHUNK, (c + 1) * GM_CHUNK)
            cols = slice(g * gd, (g + 1) * gd)
            mixed = _dot(w_g, vn[rows, cols]) + bsp_ref[:, cols]
            br_sc[0, r0 + c * GM_CHUNK:r0 + (c + 1) * GM_CHUNK, cols] = (
                u[rows, cols] * mixed).astype(BF16)
    gl = [proj(c_gate, d)]

    ll = lbl_ref[...]
    le = jnp.exp(ll - jnp.max(ll, axis=0, keepdims=True))
    lb = jnp.sum(le[:layer + 1], axis=0, keepdims=True) / jnp.sum(le, axis=0, keepdims=True)
    fg = (0.5 + 0.5 * lb) + (0.5 - 0.5 * lb) * jnp.tanh(hf)
    kk = 1.0 - fg
    logf = jnp.log(fg)
    qf = _silu_half(hq)
    cr = min(ts, V7X_MXU_DEPTH)
    rt = lax.broadcasted_iota(jnp.int32, (cr, cr), 0)
    ct = lax.broadcasted_iota(jnp.int32, (cr, cr), 1)
    cum_m = jnp.where((ct <= rt) & (rt // HG_SUB == ct // HG_SUB), 1.0, 0.0).astype(BF16)
    lf_hi = logf.astype(BF16)
    lf_lo = (logf - lf_hi.astype(F32)).astype(BF16)
    a_sub = jnp.concatenate(
        [_dot(cum_m, lf_hi[r:r + cr]) + _dot(cum_m, lf_lo[r:r + cr]) for r in range(0, ts, cr)],
        axis=0)

    xq = (p_x * (XA_DIM ** -0.5)).astype(BF16)
    att_s = [_dot(xq[:, h * XA_DIM:(h + 1) * XA_DIM], kt_sc[h * XA_DIM:(h + 1) * XA_DIM, :])
             for h in range(XA_HEADS)]
    gl.append(proj(c_gate + d, d))

    rc = lax.broadcasted_iota(jnp.int32, (HG_CHUNK, HG_CHUNK), 0)
    cc = lax.broadcasted_iota(jnp.int32, (HG_CHUNK, HG_CHUNK), 1)
    mask_same = (cc <= rc) & (rc // HG_SUB == cc // HG_SUB)
    mask_later = rc // HG_SUB > cc // HG_SUB
    vb = hi.astype(BF16)
    nsub = HG_CHUNK // HG_SUB
    chunks = []
    for c in range(ts // HG_CHUNK):
        a_blk = [a_sub[c * HG_CHUNK + j * HG_SUB:c * HG_CHUNK + (j + 1) * HG_SUB] for j in range(nsub)]
        tot = [a_j[HG_SUB - 1:HG_SUB] for a_j in a_blk]
        qd, kd, q1, k1, qi, ke = [], [], [], [], [], []
        for j in range(nsub):
            b0 = c * HG_CHUNK + j * HG_SUB
            q_j, k_j, a_j = qf[b0:b0 + HG_SUB], kk[b0:b0 + HG_SUB], a_blk[j]
            z = a_j - a_j[HG_SUB // 2 - 1:HG_SUB // 2]
            qd.append(q_j * jnp.exp(z))
            kd.append(k_j * jnp.exp(-z))
            q1_j = q_j * jnp.exp(a_j)
            k1_j = k_j * jnp.exp(tot[j] - a_j)
            q1.append(q1_j)
            k1.append(k1_j)
            before = sum(tot[:j]) if j else None
            after = sum(tot[j + 1:]) if j + 1 < nsub else None
            qi.append(q1_j if before is None else q1_j * jnp.exp(before))
            ke.append(k1_j if after is None else k1_j * jnp.exp(after))
        cat = lambda parts: jnp.concatenate(parts, axis=0).astype(BF16)
        chunks.append((cat(qd), cat(kd), cat(q1), cat(k1), cat(qi), cat(ke), jnp.exp(sum(tot))))
    pre = {}
    for c, (qd, kd, q1, k1, qi, ke, dec) in enumerate(chunks):
        rows = slice(c * HG_CHUNK, (c + 1) * HG_CHUNK)
        for h in range(HG_HEADS):
            cols = slice(h * HG_DIM, (h + 1) * HG_DIM)
            pre[c, h] = (_dot_nt(qd[:, cols], kd[:, cols]), _dot_nt(q1[:, cols], k1[:, cols]),
                         _dot_tn(vb[rows, cols], ke[:, cols]))

    for h in range(XA_HEADS):
        cols = slice(h * XA_DIM, (h + 1) * XA_DIM)
        s = att_s[h]
        e = jnp.exp(s - jnp.max(s, axis=-1, keepdims=True))
        inv = 1.0 / jnp.sum(e, axis=-1, keepdims=True)
        br_sc[2, grp, cols] = (_dot(e.astype(BF16), v_sc[:, cols]) * inv).astype(BF16)
    gl.append(proj(c_gate + 2 * d, d))

    st = list(st)
    for c, (qd, kd, q1, k1, qi, ke, dec) in enumerate(chunks):
        rows = slice(c * HG_CHUNK, (c + 1) * HG_CHUNK)
        for h in range(HG_HEADS):
            cols = slice(h * HG_DIM, (h + 1) * HG_DIM)
            s_same, s_later, upd = pre[c, h]
            sc = jnp.where(mask_same, s_same, jnp.where(mask_later, s_later, 0.0))
            o = _dot(sc.astype(BF16), vb[rows, cols]) + _dot_nt(qi[:, cols], st[h].astype(BF16))
            ho_sc[r0 + c * HG_CHUNK:r0 + (c + 1) * HG_CHUNK, cols] = o
            st[h] = st[h] * dec[:, cols] + upd
    hg_act = _silu_half(hg)
    for h in range(HG_HEADS):
        cols = slice(h * HG_DIM, (h + 1) * HG_DIM)
        o = ho_sc[grp, cols]
        on = _rms(o) * hng_ref[...]
        br_sc[1, grp, cols] = (on * hg_act[:, cols]).astype(BF16)

    merged = None
    for n in range(N_BRANCH):
        up = _dot(br_sc[n, grp], wbr_ref[n])
        term = jnp.tanh(gl[n]) * up + up
        merged = term if merged is None else merged + term
    o_ref[0, grp] = x + _dot(merged.astype(BF16), wout_ref[...])
    return st


def _ffn_kernel(final, x_ref, g2_ref, wup_ref, cw_ref, cb_ref, wdn_ref, gf_ref, o_ref, tail_sc):
    ts = x_ref.shape[1]
    dff = wdn_ref.shape[0]
    s_idx = pl.program_id(1)

    @pl.when(s_idx == 0)
    def _start_of_sequence():
        tail_sc[...] = jnp.zeros_like(tail_sc)

    tail = tail_sc[...]
    rg = ROW_GROUP
    row = lax.broadcasted_iota(jnp.int32, (rg, 1), 0)
    for r0 in range(0, ts, rg):
        x = x_ref[0, r0:r0 + rg]
        hb = (_rms(x) * g2_ref[...]).astype(BF16)
        a = _dot(hb, wup_ref[:, :dff])
        b = _dot(hb, wup_ref[:, dff:])
        prev1 = tail[7:8, :]
        prev2 = tail[6:7, :]
        a1 = jnp.where(row == 0, prev1, pltpu.roll(a, 1, axis=0))
        a2 = jnp.where(row == 0, prev2, jnp.where(row == 1, prev1, pltpu.roll(a, 2, axis=0)))
        tail = a[rg - 8:, :]
        ac = cb_ref[...] + cw_ref[0:1, :] * a2 + cw_ref[1:2, :] * a1 + cw_ref[2:3, :] * a
        y = (_silu_half(ac) * b).astype(BF16)
        out = x + _dot(y, wdn_ref[...])
        if final:
            out = _rms(out) * gf_ref[...]
        o_ref[0, r0:r0 + rg] = out
    tail_sc[...] = tail


def _const_spec(shape):
    return pl.BlockSpec(shape, lambda b, s: (0,) * len(shape), pipeline_mode=pl.Buffered(1))


def _cast_block_spec(n_rows, n_cols, n_seq, n_steps):
    rows = next(r for r in range(BF16_ROWS, n_rows + 1, BF16_ROWS)
                if n_rows % r == 0 and n_rows // r <= n_steps)
    last = n_rows // rows - 1
    return pl.BlockSpec((rows, n_cols), lambda b, s: (jnp.minimum(b * n_seq + s, last), 0))


def _mixer_call(layer, x, mem, g1, w_in, lng, lnb, wsp, bsp, lbl, hng, mg, wkv, wbr, wout,
                wup_f32, wdn_f32):
    bsz, seq, d = x.shape
    mlen = mem.shape[1]
    ts = SEQ_TILE
    n_seq = seq // ts
    consts = (g1, w_in, lng, lnb, wsp, bsp, lbl, hng, mg, wkv, wbr, wout)
    cast_specs = [_cast_block_spec(*w.shape, n_seq, bsz * n_seq) for w in (wup_f32, wdn_f32)]
    return pl.pallas_call(
        functools.partial(_mixer_kernel, layer),
        grid=(bsz, n_seq),
        in_specs=[pl.BlockSpec((1, ts, d), lambda b, s: (b, s, 0)),
                  pl.BlockSpec((1, mlen, d), lambda b, s: (b, 0, 0))]
                 + [_const_spec(c.shape) for c in consts] + cast_specs,
        out_specs=[pl.BlockSpec((1, ts, d), lambda b, s: (b, s, 0))] + cast_specs,
        out_shape=[jax.ShapeDtypeStruct(x.shape, F32),
                   jax.ShapeDtypeStruct(wup_f32.shape, BF16),
                   jax.ShapeDtypeStruct(wdn_f32.shape, BF16)],
        scratch_shapes=[pltpu.VMEM((XA_WIDTH, mlen), BF16),
                        pltpu.VMEM((mlen, XA_WIDTH), BF16),
                        pltpu.VMEM((HG_HEADS, HG_DIM, HG_DIM), F32),
                        pltpu.VMEM((N_BRANCH, ts, BRANCH_WIDTH), BF16),
                        pltpu.VMEM((ts, HG_WIDTH), F32)],
        compiler_params=pltpu.CompilerParams(
            dimension_semantics=("arbitrary", "arbitrary"),
            vmem_limit_bytes=V7X_VMEM_LIMIT_BYTES),
        name="mixer",
    )(x, mem, *consts, wup_f32, wdn_f32)


def _ffn_call(final, x, g2, wup, cw, cb, wdn, gf):
    bsz, seq, d = x.shape
    dff = wdn.shape[0]
    ts = SEQ_TILE
    consts = (g2, wup, cw, cb, wdn, gf)
    return pl.pallas_call(
        functools.partial(_ffn_kernel, final),
        grid=(bsz, seq // ts),
        in_specs=[pl.BlockSpec((1, ts, d), lambda b, s: (b, s, 0))]
                 + [_const_spec(c.shape) for c in consts],
        out_specs=pl.BlockSpec((1, ts, d), lambda b, s: (b, s, 0)),
        out_shape=jax.ShapeDtypeStruct(x.shape, F32),
        scratch_shapes=[pltpu.VMEM((8, dff), F32)],
        compiler_params=pltpu.CompilerParams(
            dimension_semantics=("parallel", "arbitrary"),
            vmem_limit_bytes=V7X_VMEM_LIMIT_BYTES),
        name="ffn",
    )(x, *consts)


def kernel(x, mem, norm1_g, w_in, ln_v_g, ln_v_b, w_spatial, b_spatial, lb_logits, hgrn_norm_g,
           mem_norm_g, w_mem_kv, w_branch, w_out, norm2_g, w_up, conv_w, conv_b, w_down, final_g):
    depth = w_in.shape[0]
    assert x.shape[1] % SEQ_TILE == 0 and SEQ_TILE % ROW_GROUP == 0
    assert ROW_GROUP % GM_CHUNK == 0 and ROW_GROUP % HG_CHUNK == 0
    assert HG_CHUNK % HG_SUB == 0 and HG_SUB % 16 == 0
    row = lambda p: p.reshape(1, -1)
    gd = GM_WIDTH // GM_GROUPS
    d = x.shape[-1]
    sections = ((GM_WIDTH, 0.5), (GM_WIDTH, 0.5), (HG_WIDTH, 0.5), (HG_WIDTH, 0.5), (HG_WIDTH, 1.0),
                (HG_WIDTH, 0.5), (XA_WIDTH, 1.0), (N_BRANCH * d, 0.5))
    in_scale = jnp.concatenate([jnp.full((1, w), s, F32) for w, s in sections], axis=1)
    for l in range(depth):
        bsp = jnp.repeat(b_spatial[l].T, gd, axis=1)
        x, wup, wdn = _mixer_call(
            l, x, mem, row(norm1_g[l]), (w_in[l] * in_scale).astype(BF16),
            row(ln_v_g[l]), row(ln_v_b[l]), w_spatial[l].astype(BF16), bsp, lb_logits,
            row(hgrn_norm_g[l]), row(mem_norm_g[l]), w_mem_kv[l].astype(BF16),
            (0.5 * w_branch[l]).astype(BF16), w_out[l].astype(BF16), w_up[l], w_down[l])
        x = _ffn_call(l == depth - 1, x, row(norm2_g[l]), wup, 0.5 * conv_w[l],
                      row(0.5 * conv_b[l]), wdn, row(final_g))
    return x
```

```python
import functools

import jax
import jax.numpy as jnp
from jax import lax
from jax.experimental import pallas as pl
from jax.experimental.pallas import tpu as pltpu

F32 = jnp.float32
BF16 = jnp.bfloat16

EPS = 1e-6
GM_GROUPS = 4
GM_CHUNK = 128
GM_WIDTH = 512
HG_HEADS = 4
HG_DIM = 128
HG_WIDTH = HG_HEADS * HG_DIM
HG_CHUNK = 64
HG_SUB = 32
XA_HEADS = 4
XA_DIM = 128
XA_WIDTH = XA_HEADS * XA_DIM
N_BRANCH = 3
BRANCH_WIDTH = 512

ROW_GROUP = 256
SEQ_TILE = 1024
V7X_VMEM_LIMIT_BYTES = 56 * 1024 * 1024
BF16_ROWS = 16
F32_ROWS = 8


def _dot(a, b):
    return jnp.dot(a, b, preferred_element_type=F32)


def _dot_nt(a, b):
    return lax.dot_general(a, b, (((1,), (1,)), ((), ())), preferred_element_type=F32)


def _dot_tn(a, b):
    return lax.dot_general(a, b, (((0,), (0,)), ((), ())), preferred_element_type=F32)


def _silu_half(h):
    return h * jnp.tanh(h) + h


GELU_K0 = 2.0 * 0.7978845608028654
GELU_K1 = 8.0 * 0.044715 * 0.7978845608028654


def _gelu_half(h):
    return h * jnp.tanh(h * (GELU_K0 + GELU_K1 * (h * h))) + h


def _rms(x):
    return x * lax.rsqrt(jnp.mean(x * x, axis=-1, keepdims=True) + EPS)


def _mixer_kernel(layer, x_ref, mem_ref, g1_ref, w_in_ref, lng_ref, lnb_ref, wsp_ref, bsp_ref,
                  lbl_ref, hng_ref, mg_ref, wkv_ref, wbr_ref, wout_ref, wup_f32_ref, wdn_f32_ref,
                  o_ref, wup_bf16_ref, wdn_bf16_ref, kt_sc, v_sc, st_sc, br_sc, ho_sc):
    ts = x_ref.shape[1]
    s_idx = pl.program_id(1)

    wup_bf16_ref[...] = wup_f32_ref[...].astype(BF16)
    wdn_bf16_ref[...] = wdn_f32_ref[...].astype(BF16)

    @pl.when(s_idx == 0)
    def _start_of_sequence():
        m = mem_ref[0]
        mn = (_rms(m) * mg_ref[...]).astype(BF16)
        kv = _dot(mn, wkv_ref[...])
        kt_sc[...] = kv[:, :XA_WIDTH].T.astype(BF16)
        v_sc[...] = kv[:, XA_WIDTH:].astype(BF16)
        st_sc[...] = jnp.zeros_like(st_sc)

    st = [st_sc[h] for h in range(HG_HEADS)]
    for r0 in range(0, ts, ROW_GROUP):
        st = _mixer_rows(layer, r0, st, x_ref, g1_ref, w_in_ref, lng_ref, lnb_ref, wsp_ref, bsp_ref,
                         lbl_ref, hng_ref, wbr_ref, wout_ref, o_ref, kt_sc, v_sc, br_sc, ho_sc)
    for h in range(HG_HEADS):
        st_sc[h] = st[h]


def _mixer_rows(layer, r0, st, x_ref, g1_ref, w_in_ref, lng_ref, lnb_ref, wsp_ref, bsp_ref,
                lbl_ref, hng_ref, wbr_ref, wout_ref, o_ref, kt_sc, v_sc, br_sc, ho_sc):
    ts = ROW_GROUP
    grp = slice(r0, r0 + ts)
    x = x_ref[0, grp]
    hb = (_rms(x) * g1_ref[...]).astype(BF16)
    d = x.shape[-1]

    def proj(lo, width):
        return _dot(hb, w_in_ref[:, lo:lo + width])

    c_u, c_v = 0, GM_WIDTH
    c_q = 2 * GM_WIDTH
    c_f, c_i, c_g = c_q + HG_WIDTH, c_q + 2 * HG_WIDTH, c_q + 3 * HG_WIDTH
    c_x = c_q + 4 * HG_WIDTH
    c_gate = c_x + XA_WIDTH

    p_u = proj(c_u, GM_WIDTH)
    p_v = proj(c_v, GM_WIDTH)
    hq = proj(c_q, HG_WIDTH)
    hf = proj(c_f, HG_WIDTH)
    hi = proj(c_i, HG_WIDTH)
    hg = proj(c_g, HG_WIDTH)

    u = _gelu_half(p_u)
    v = _gelu_half(p_v)
    mu = jnp.mean(v, axis=-1, keepdims=True)
    vc = v - mu
    var = jnp.mean(vc * vc, axis=-1, keepdims=True)
    vn = ((vc * lax.rsqrt(var + EPS)) * lng_ref[...] + lnb_ref[...]).astype(BF16)
    p_x = proj(c_x, XA_WIDTH)
    r128 = lax.broadcasted_iota(jnp.int32, (GM_CHUNK, GM_CHUNK), 0)
    c128 = lax.broadcasted_iota(jnp.int32, (GM_CHUNK, GM_CHUNK), 1)
    tril128 = c128 <= r128
    gd = GM_WIDTH // GM_GROUPS
    for g in range(GM_GROUPS):
        w_g = jnp.where(tril128, wsp_ref[g], jnp.zeros((), BF16))
        for c in range(ts // GM_CHUNK):
            rows = slice(c * GM_CHUNK, (c + 1) * GM_CHUNK)
            cols = slice(g * gd, (g + 1) * gd)
            mixed = _dot(w_g, vn[rows, cols]) + bsp_ref[:, cols]
            br_sc[0, r0 + c * GM_CHUNK:r0 + (c + 1) * GM_CHUNK, cols] = (
                u[rows, cols] * mixed).astype(BF16)
    gl = [proj(c_gate, d)]

    ll = lbl_ref[...]
    le = jnp.exp(ll - jnp.max(ll, axis=0, keepdims=True))
    lb = jnp.sum(le[:layer + 1], axis=0, keepdims=True) / jnp.sum(le, axis=0, keepdims=True)
    fg = (0.5 + 0.5 * lb) + (0.5 - 0.5 * lb) * jnp.tanh(hf)
    kk = 1.0 - fg
    logf = jnp.log(fg)
    qf = _silu_half(hq)
    sub_row = lax.broadcasted_iota(jnp.int32, (ts, 1), 0) % HG_SUB
    a_sub = logf
    shift = 1
    while shift < HG_SUB:
        a_sub = a_sub + jnp.where(sub_row >= shift, pltpu.roll(a_sub, shift, axis=0), 0.0)
        shift *= 2

    xq = (p_x * (XA_DIM ** -0.5)).astype(BF16)
    att_s = [_dot(xq[:, h * XA_DIM:(h + 1) * XA_DIM], kt_sc[h * XA_DIM:(h + 1) * XA_DIM, :])
             for h in range(XA_HEADS)]
    gl.append(proj(c_gate + d, d))

    rc = lax.broadcasted_iota(jnp.int32, (HG_CHUNK, HG_CHUNK), 0)
    cc = lax.broadcasted_iota(jnp.int32, (HG_CHUNK, HG_CHUNK), 1)
    mask_same = (cc <= rc) & (rc // HG_SUB == cc // HG_SUB)
    mask_later = rc // HG_SUB > cc // HG_SUB
    vb = hi.astype(BF16)
    nsub = HG_CHUNK // HG_SUB
    chunks = []
    for c in range(ts // HG_CHUNK):
        a_blk = [a_sub[c * HG_CHUNK + j * HG_SUB:c * HG_CHUNK + (j + 1) * HG_SUB] for j in range(nsub)]
        tot = [a_j[HG_SUB - 1:HG_SUB] for a_j in a_blk]
        qd, kd, q1, k1, qi, ke = [], [], [], [], [], []
        for j in range(nsub):
            b0 = c * HG_CHUNK + j * HG_SUB
            q_j, k_j, a_j = qf[b0:b0 + HG_SUB], kk[b0:b0 + HG_SUB], a_blk[j]
            z = a_j - a_j[HG_SUB // 2 - 1:HG_SUB // 2]
            qd.append(q_j * jnp.exp(z))
            kd.append(k_j * jnp.exp(-z))
            q1_j = q_j * jnp.exp(a_j)
            k1_j = k_j * jnp.exp(tot[j] - a_j)
            q1.append(q1_j)
            k1.append(k1_j)
            before = sum(tot[:j]) if j else None
            after = sum(tot[j + 1:]) if j + 1 < nsub else None
            qi.append(q1_j if before is None else q1_j * jnp.exp(before))
            ke.append(k1_j if after is None else k1_j * jnp.exp(after))
        cat = lambda parts: jnp.concatenate(parts, axis=0).astype(BF16)
        chunks.append((cat(qd), cat(kd), cat(q1), cat(k1), cat(qi), cat(ke), sum(tot)))
    pre = {}
    for c, (qd, kd, q1, k1, qi, ke, _) in enumerate(chunks):
        rows = slice(c * HG_CHUNK, (c + 1) * HG_CHUNK)
        for h in range(HG_HEADS):
            cols = slice(h * HG_DIM, (h + 1) * HG_DIM)
            pre[c, h] = (_dot_nt(qd[:, cols], kd[:, cols]), _dot_nt(q1[:, cols], k1[:, cols]),
                         _dot_tn(ke[:, cols], vb[rows, cols]))

    for h in range(XA_HEADS):
        cols = slice(h * XA_DIM, (h + 1) * XA_DIM)
        s = att_s[h]
        e = jnp.exp(s - jnp.max(s, axis=-1, keepdims=True))
        inv = 1.0 / jnp.sum(e, axis=-1, keepdims=True)
        br_sc[2, grp, cols] = (_dot(e.astype(BF16), v_sc[:, cols]) * inv).astype(BF16)
    gl.append(proj(c_gate + 2 * d, d))

    st = list(st)
    log_dec = [ch[6] for ch in chunks]
    log_dec += [jnp.zeros_like(log_dec[0])] * (-len(log_dec) % F32_ROWS)
    dec_t = jnp.exp(jnp.concatenate(log_dec, axis=0)).T
    for c, (qd, kd, q1, k1, qi, ke, _) in enumerate(chunks):
        rows = slice(c * HG_CHUNK, (c + 1) * HG_CHUNK)
        for h in range(HG_HEADS):
            cols = slice(h * HG_DIM, (h + 1) * HG_DIM)
            s_same, s_later, upd = pre[c, h]
            sc = jnp.where(mask_same, s_same, jnp.where(mask_later, s_later, 0.0))
            o = _dot(sc.astype(BF16), vb[rows, cols]) + _dot(qi[:, cols], st[h].astype(BF16))
            ho_sc[r0 + c * HG_CHUNK:r0 + (c + 1) * HG_CHUNK, cols] = o
            st[h] = st[h] * dec_t[cols, c:c + 1] + upd
    hg_act = _silu_half(hg)
    for h in range(HG_HEADS):
        cols = slice(h * HG_DIM, (h + 1) * HG_DIM)
        o = ho_sc[grp, cols]
        on = _rms(o) * hng_ref[...]
        br_sc[1, grp, cols] = (on * hg_act[:, cols]).astype(BF16)

    merged = None
    for n in range(N_BRANCH):
        up = _dot(br_sc[n, grp], wbr_ref[n])
        term = jnp.tanh(gl[n]) * up + up
        merged = term if merged is None else merged + term
    o_ref[0, grp] = x + _dot(merged.astype(BF16), wout_ref[...])
    return st


def _ffn_kernel(final, x_ref, g2_ref, wup_ref, cw_ref, cb_ref, wdn_ref, gf_ref, o_ref, tail_sc):
    ts = x_ref.shape[1]
    dff = wdn_ref.shape[0]
    s_idx = pl.program_id(1)

    @pl.when(s_idx == 0)
    def _start_of_sequence():
        tail_sc[...] = jnp.zeros_like(tail_sc)

    tail = tail_sc[...]
    rg = ROW_GROUP
    row = lax.broadcasted_iota(jnp.int32, (rg, 1), 0)
    for r0 in range(0, ts, rg):
        x = x_ref[0, r0:r0 + rg]
        hb = (_rms(x) * g2_ref[...]).astype(BF16)
        a = _dot(hb, wup_ref[:, :dff])
        b = _dot(hb, wup_ref[:, dff:])
        prev1 = tail[7:8, :]
        prev2 = tail[6:7, :]
        a1 = jnp.where(row == 0, prev1, pltpu.roll(a, 1, axis=0))
        a2 = jnp.where(row == 0, prev2, jnp.where(row == 1, prev1, pltpu.roll(a, 2, axis=0)))
        tail = a[rg - 8:, :]
        ac = cb_ref[...] + cw_ref[0:1, :] * a2 + cw_ref[1:2, :] * a1 + cw_ref[2:3, :] * a
        y = (_silu_half(ac) * b).astype(BF16)
        out = x + _dot(y, wdn_ref[...])
        if final:
            out = _rms(out) * gf_ref[...]
        o_ref[0, r0:r0 + rg] = out
    tail_sc[...] = tail


def _const_spec(shape):
    return pl.BlockSpec(shape, lambda b, s: (0,) * len(shape), pipeline_mode=pl.Buffered(1))


def _cast_block_spec(n_rows, n_cols, n_seq, n_steps):
    rows = next(r for r in range(BF16_ROWS, n_rows + 1, BF16_ROWS)
                if n_rows % r == 0 and n_rows // r <= n_steps)
    last = n_rows // rows - 1
    return pl.BlockSpec((rows, n_cols), lambda b, s: (jnp.minimum(b * n_seq + s, last), 0))


def _mixer_call(layer, x, mem, g1, w_in, lng, lnb, wsp, bsp, lbl, hng, mg, wkv, wbr, wout,
                wup_f32, wdn_f32):
    bsz, seq, d = x.shape
    mlen = mem.shape[1]
    ts = SEQ_TILE
    n_seq = seq // ts
    consts = (g1, w_in, lng, lnb, wsp, bsp, lbl, hng, mg, wkv, wbr, wout)
    cast_specs = [_cast_block_spec(*w.shape, n_seq, bsz * n_seq) for w in (wup_f32, wdn_f32)]
    return pl.pallas_call(
        functools.partial(_mixer_kernel, layer),
        grid=(bsz, n_seq),
        in_specs=[pl.BlockSpec((1, ts, d), lambda b, s: (b, s, 0)),
                  pl.BlockSpec((1, mlen, d), lambda b, s: (b, 0, 0))]
                 + [_const_spec(c.shape) for c in consts] + cast_specs,
        out_specs=[pl.BlockSpec((1, ts, d), lambda b, s: (b, s, 0))] + cast_specs,
        out_shape=[jax.ShapeDtypeStruct(x.shape, F32),
                   jax.ShapeDtypeStruct(wup_f32.shape, BF16),
                   jax.ShapeDtypeStruct(wdn_f32.shape, BF16)],
        scratch_shapes=[pltpu.VMEM((XA_WIDTH, mlen), BF16),
                        pltpu.VMEM((mlen, XA_WIDTH), BF16),
                        pltpu.VMEM((HG_HEADS, HG_DIM, HG_DIM), F32),
                        pltpu.VMEM((N_BRANCH, ts, BRANCH_WIDTH), BF16),
                        pltpu.VMEM((ts, HG_WIDTH), F32)],
        compiler_params=pltpu.CompilerParams(
            dimension_semantics=("arbitrary", "arbitrary"),
            vmem_limit_bytes=V7X_VMEM_LIMIT_BYTES),
        name="mixer",
    )(x, mem, *consts, wup_f32, wdn_f32)


def _ffn_call(final, x, g2, wup, cw, cb, wdn, gf):
    bsz, seq, d = x.shape
    dff = wdn.shape[0]
    ts = SEQ_TILE
    consts = (g2, wup, cw, cb, wdn, gf)
    return pl.pallas_call(
        functools.partial(_ffn_kernel, final),
        grid=(bsz, seq // ts),
        in_specs=[pl.BlockSpec((1, ts, d), lambda b, s: (b, s, 0))]
                 + [_const_spec(c.shape) for c in consts],
        out_specs=pl.BlockSpec((1, ts, d), lambda b, s: (b, s, 0)),
        out_shape=jax.ShapeDtypeStruct(x.shape, F32),
        scratch_shapes=[pltpu.VMEM((8, dff), F32)],
        compiler_params=pltpu.CompilerParams(
            dimension_semantics=("parallel", "arbitrary"),
            vmem_limit_bytes=V7X_VMEM_LIMIT_BYTES),
        name="ffn",
    )(x, *consts)


def kernel(x, mem, norm1_g, w_in, ln_v_g, ln_v_b, w_spatial, b_spatial, lb_logits, hgrn_norm_g,
           mem_norm_g, w_mem_kv, w_branch, w_out, norm2_g, w_up, conv_w, conv_b, w_down, final_g):
    depth = w_in.shape[0]
    assert x.shape[1] % SEQ_TILE == 0 and SEQ_TILE % ROW_GROUP == 0
    assert ROW_GROUP % GM_CHUNK == 0 and ROW_GROUP % HG_CHUNK == 0
    assert HG_CHUNK % HG_SUB == 0 and HG_SUB % 16 == 0
    row = lambda p: p.reshape(1, -1)
    gd = GM_WIDTH // GM_GROUPS
    d = x.shape[-1]
    sections = ((GM_WIDTH, 0.5), (GM_WIDTH, 0.5), (HG_WIDTH, 0.5), (HG_WIDTH, 0.5), (HG_WIDTH, 1.0),
                (HG_WIDTH, 0.5), (XA_WIDTH, 1.0), (N_BRANCH * d, 0.5))
    in_scale = jnp.concatenate([jnp.full((1, w), s, F32) for w, s in sections], axis=1)
    for l in range(depth):
        bsp = jnp.repeat(b_spatial[l].T, gd, axis=1)
        x, wup, wdn = _mixer_call(
            l, x, mem, row(norm1_g[l]), (w_in[l] * in_scale).astype(BF16),
            row(ln_v_g[l]), row(ln_v_b[l]), w_spatial[l].astype(BF16), bsp, lb_logits,
            row(hgrn_norm_g[l]), row(mem_norm_g[l]), w_mem_kv[l].astype(BF16),
            (0.5 * w_branch[l]).astype(BF16), w_out[l].astype(BF16), w_up[l], w_down[l])
        x = _ffn_call(l == depth - 1, x, row(norm2_g[l]), wup, 0.5 * conv_w[l],
                      row(0.5 * conv_b[l]), wdn, row(final_g))
    return x
```

```python
import functools

import jax
import jax.numpy as jnp
from jax import lax
from jax.experimental import pallas as pl
from jax.experimental.pallas import tpu as pltpu

F32 = jnp.float32
BF16 = jnp.bfloat16

EPS = 1e-6
GM_GROUPS = 4
GM_CHUNK = 128
GM_WIDTH = 512
HG_HEADS = 4
HG_DIM = 128
HG_WIDTH = HG_HEADS * HG_DIM
HG_CHUNK = 64
HG_SUB = 32
XA_HEADS = 4
XA_DIM = 128
XA_WIDTH = XA_HEADS * XA_DIM
N_BRANCH = 3
BRANCH_WIDTH = 512

ROW_GROUP = 256
FFN_ROW_GROUP = 512
SEQ_TILE = 1024
V7X_VMEM_LIMIT_BYTES = 56 * 1024 * 1024
V7X_MXU_DEPTH = 256
BF16_ROWS = 16
F32_ROWS = 8


def _dot(a, b):
    return jnp.dot(a, b, preferred_element_type=F32)


def _dot_nt(a, b):
    return lax.dot_general(a, b, (((1,), (1,)), ((), ())), preferred_element_type=F32)


def _dot_tn(a, b):
    return lax.dot_general(a, b, (((0,), (0,)), ((), ())), preferred_element_type=F32)


def _silu_half(h):
    return h * jnp.tanh(h) + h


GELU_K0 = 2.0 * 0.7978845608028654
GELU_K1 = 8.0 * 0.044715 * 0.7978845608028654


def _gelu_half(h):
    return h * jnp.tanh(h * (GELU_K0 + GELU_K1 * (h * h))) + h


def _rms(x):
    return x * lax.rsqrt(jnp.mean(x * x, axis=-1, keepdims=True) + EPS)


def _mixer_kernel(layer, x_ref, mem_ref, g1_ref, w_in_ref, lng_ref, lnb_ref, wsp_ref, bsp_ref,
                  lbl_ref, hng_ref, mg_ref, wkv_ref, wbr_ref, wout_ref, wup_f32_ref, wdn_f32_ref,
                  o_ref, wup_bf16_ref, wdn_bf16_ref, kt_sc, v_sc, st_sc, br_sc, ho_sc):
    ts = x_ref.shape[1]
    s_idx = pl.program_id(1)

    wup_bf16_ref[...] = wup_f32_ref[...].astype(BF16)
    wdn_bf16_ref[...] = wdn_f32_ref[...].astype(BF16)

    @pl.when(s_idx == 0)
    def _start_of_sequence():
        m = mem_ref[0]
        mn = (_rms(m) * mg_ref[...]).astype(BF16)
        kv = _dot(mn, wkv_ref[...])
        kt_sc[...] = kv[:, :XA_WIDTH].T.astype(BF16)
        v_sc[...] = kv[:, XA_WIDTH:].astype(BF16)
        st_sc[...] = jnp.zeros_like(st_sc)

    st = [st_sc[h] for h in range(HG_HEADS)]
    for r0 in range(0, ts, ROW_GROUP):
        st = _mixer_rows(layer, r0, st, x_ref, g1_ref, w_in_ref, lng_ref, lnb_ref, wsp_ref, bsp_ref,
                         lbl_ref, hng_ref, wbr_ref, wout_ref, o_ref, kt_sc, v_sc, br_sc, ho_sc)
    for h in range(HG_HEADS):
        st_sc[h] = st[h]


def _mixer_rows(layer, r0, st, x_ref, g1_ref, w_in_ref, lng_ref, lnb_ref, wsp_ref, bsp_ref,
                lbl_ref, hng_ref, wbr_ref, wout_ref, o_ref, kt_sc, v_sc, br_sc, ho_sc):
    ts = ROW_GROUP
    grp = slice(r0, r0 + ts)
    x = x_ref[0, grp]
    hb = (_rms(x) * g1_ref[...]).astype(BF16)
    d = x.shape[-1]

    def proj(lo, width):
        return _dot(hb, w_in_ref[:, lo:lo + width])

    c_u, c_v = 0, GM_WIDTH
    c_q = 2 * GM_WIDTH
    c_f, c_i, c_g = c_q + HG_WIDTH, c_q + 2 * HG_WIDTH, c_q + 3 * HG_WIDTH
    c_x = c_q + 4 * HG_WIDTH
    c_gate = c_x + XA_WIDTH

    p_u = proj(c_u, GM_WIDTH)
    p_v = proj(c_v, GM_WIDTH)
    hq = proj(c_q, HG_WIDTH)
    hf = proj(c_f, HG_WIDTH)
    hi = proj(c_i, HG_WIDTH)
    hg = proj(c_g, HG_WIDTH)

    u = _gelu_half(p_u)
    v = _gelu_half(p_v)
    mu = jnp.mean(v, axis=-1, keepdims=True)
    vc = v - mu
    var = jnp.mean(vc * vc, axis=-1, keepdims=True)
    vn = ((vc * lax.rsqrt(var + EPS)) * lng_ref[...] + lnb_ref[...]).astype(BF16)
    p_x = proj(c_x, XA_WIDTH)
    r128 = lax.broadcasted_iota(jnp.int32, (GM_CHUNK, GM_CHUNK), 0)
    c128 = lax.broadcasted_iota(jnp.int32, (GM_CHUNK, GM_CHUNK), 1)
    tril128 = c128 <= r128
    gd = GM_WIDTH // GM_GROUPS
    for g in range(GM_GROUPS):
        w_g = jnp.where(tril128, wsp_ref[g], jnp.zeros((), BF16))
        for c in range(ts // GM_CHUNK):
            rows = slice(c * GM_CHUNK, (c + 1) * GM_CHUNK)
            cols = slice(g * gd, (g + 1) * gd)
            mixed = _dot(w_g, vn[rows, cols]) + bsp_ref[:, cols]
            br_sc[0, r0 + c * GM_CHUNK:r0 + (c + 1) * GM_CHUNK, cols] = (
                u[rows, cols] * mixed).astype(BF16)
    gl = [proj(c_gate, d)]

    ll = lbl_ref[...]
    le = jnp.exp(ll - jnp.max(ll, axis=0, keepdims=True))
    lb = jnp.sum(le[:layer + 1], axis=0, keepdims=True) / jnp.sum(le, axis=0, keepdims=True)
    fg = (0.5 + 0.5 * lb) + (0.5 - 0.5 * lb) * jnp.tanh(hf)
    kk = 1.0 - fg
    logf = jnp.log(fg)
    qf = _silu_half(hq)
    cr = min(ts, V7X_MXU_DEPTH)
    rt = lax.broadcasted_iota(jnp.int32, (cr, cr), 0)
    ct = lax.broadcasted_iota(jnp.int32, (cr, cr), 1)
    cum_m = jnp.where((ct <= rt) & (rt // HG_SUB == ct // HG_SUB), 1.0, 0.0).astype(BF16)
    lf_hi = logf.astype(BF16)
    lf_lo = (logf - lf_hi.astype(F32)).astype(BF16)
    a_sub = jnp.concatenate(
        [_dot(cum_m, lf_hi[r:r + cr]) + _dot(cum_m, lf_lo[r:r + cr]) for r in range(0, ts, cr)],
        axis=0)

    xq = (p_x * (XA_DIM ** -0.5)).astype(BF16)
    att_s = [_dot(xq[:, h * XA_DIM:(h + 1) * XA_DIM], kt_sc[h * XA_DIM:(h + 1) * XA_DIM, :])
             for h in range(XA_HEADS)]
    gl.append(proj(c_gate + d, d))

    rc = lax.broadcasted_iota(jnp.int32, (HG_CHUNK, HG_CHUNK), 0)
    cc = lax.broadcasted_iota(jnp.int32, (HG_CHUNK, HG_CHUNK), 1)
    mask_same = (cc <= rc) & (rc // HG_SUB == cc // HG_SUB)
    mask_later = rc // HG_SUB > cc // HG_SUB
    vb = hi.astype(BF16)
    nsub = HG_CHUNK // HG_SUB
    chunks = []
    for c in range(ts // HG_CHUNK):
        a_blk = [a_sub[c * HG_CHUNK + j * HG_SUB:c * HG_CHUNK + (j + 1) * HG_SUB] for j in range(nsub)]
        tot = [a_j[HG_SUB - 1:HG_SUB] for a_j in a_blk]
        qd, kd, q1, k1, qi, ke = [], [], [], [], [], []
        for j in range(nsub):
            b0 = c * HG_CHUNK + j * HG_SUB
            q_j, k_j, a_j = qf[b0:b0 + HG_SUB], kk[b0:b0 + HG_SUB], a_blk[j]
            z = a_j - a_j[HG_SUB // 2 - 1:HG_SUB // 2]
            qd.append(q_j * jnp.exp(z))
            kd.append(k_j * jnp.exp(-z))
            q1_j = q_j * jnp.exp(a_j)
            k1_j = k_j * jnp.exp(tot[j] - a_j)
            q1.append(q1_j)
            k1.append(k1_j)
            before = sum(tot[:j]) if j else None
            after = sum(tot[j + 1:]) if j + 1 < nsub else None
            qi.append(q1_j if before is None else q1_j * jnp.exp(before))
            ke.append(k1_j if after is None else k1_j * jnp.exp(after))
        cat = lambda parts: jnp.concatenate(parts, axis=0).astype(BF16)
        chunks.append((cat(qd), cat(kd), cat(q1), cat(k1), cat(qi), cat(ke), sum(tot)))
    pre = {}
    for c, (qd, kd, q1, k1, qi, ke, _) in enumerate(chunks):
        rows = slice(c * HG_CHUNK, (c + 1) * HG_CHUNK)
        for h in range(HG_HEADS):
            cols = slice(h * HG_DIM, (h + 1) * HG_DIM)
            pre[c, h] = (_dot_nt(qd[:, cols], kd[:, cols]), _dot_nt(q1[:, cols], k1[:, cols]),
                         _dot_tn(ke[:, cols], vb[rows, cols]))

    for h in range(XA_HEADS):
        cols = slice(h * XA_DIM, (h + 1) * XA_DIM)
        s = att_s[h]
        e = jnp.exp(s - jnp.max(s, axis=-1, keepdims=True))
        inv = 1.0 / jnp.sum(e, axis=-1, keepdims=True)
        br_sc[2, grp, cols] = (_dot(e.astype(BF16), v_sc[:, cols]) * inv).astype(BF16)
    gl.append(proj(c_gate + 2 * d, d))

    st = list(st)
    log_dec = [ch[6] for ch in chunks]
    log_dec += [jnp.zeros_like(log_dec[0])] * (-len(log_dec) % F32_ROWS)
    dec_t = jnp.exp(jnp.concatenate(log_dec, axis=0)).T
    for c, (qd, kd, q1, k1, qi, ke, _) in enumerate(chunks):
        rows = slice(c * HG_CHUNK, (c + 1) * HG_CHUNK)
        for h in range(HG_HEADS):
            cols = slice(h * HG_DIM, (h + 1) * HG_DIM)
            s_same, s_later, upd = pre[c, h]
            sc = jnp.where(mask_same, s_same, jnp.where(mask_later, s_later, 0.0))
            o = _dot(sc.astype(BF16), vb[rows, cols]) + _dot(qi[:, cols], st[h].astype(BF16))
            ho_sc[r0 + c * HG_CHUNK:r0 + (c + 1) * HG_CHUNK, cols] = o
            st[h] = st[h] * dec_t[cols, c:c + 1] + upd
    hg_act = _silu_half(hg)
    for h in range(HG_HEADS):
        cols = slice(h * HG_DIM, (h + 1) * HG_DIM)
        o = ho_sc[grp, cols]
        on = _rms(o) * hng_ref[...]
        br_sc[1, grp, cols] = (on * hg_act[:, cols]).astype(BF16)

    merged = None
    for n in range(N_BRANCH):
        up = _dot(br_sc[n, grp], wbr_ref[n])
        term = jnp.tanh(gl[n]) * up + up
        merged = term if merged is None else merged + term
    o_ref[0, grp] = x + _dot(merged.astype(BF16), wout_ref[...])
    return st


def _ffn_kernel(final, x_ref, g2_ref, wup_ref, cw_ref, cb_ref, wdn_ref, gf_ref, o_ref, tail_sc):
    ts = x_ref.shape[1]
    dff = wdn_ref.shape[0]
    s_idx = pl.program_id(1)

    @pl.when(s_idx == 0)
    def _start_of_sequence():
        tail_sc[...] = jnp.zeros_like(tail_sc)

    tail = tail_sc[...]
    rg = FFN_ROW_GROUP
    row = lax.broadcasted_iota(jnp.int32, (rg, 1), 0)
    for r0 in range(0, ts, rg):
        x = x_ref[0, r0:r0 + rg]
        hb = (_rms(x) * g2_ref[...]).astype(BF16)
        a = _dot(hb, wup_ref[:, :dff])
        b = _dot(hb, wup_ref[:, dff:])
        prev1 = tail[7:8, :]
        prev2 = tail[6:7, :]
        a1 = jnp.where(row == 0, prev1, pltpu.roll(a, 1, axis=0))
        a2 = jnp.where(row == 0, prev2, jnp.where(row == 1, prev1, pltpu.roll(a, 2, axis=0)))
        tail = a[rg - 8:, :]
        ac = cb_ref[...] + cw_ref[0:1, :] * a2 + cw_ref[1:2, :] * a1 + cw_ref[2:3, :] * a
        y = (_silu_half(ac) * b).astype(BF16)
        out = x + _dot(y, wdn_ref[...])
        if final:
            out = _rms(out) * gf_ref[...]
        o_ref[0, r0:r0 + rg] = out
    tail_sc[...] = tail


def _const_spec(shape):
    return pl.BlockSpec(shape, lambda b, s: (0,) * len(shape), pipeline_mode=pl.Buffered(1))


def _cast_block_spec(n_rows, n_cols, n_seq, n_steps):
    rows = next(r for r in range(BF16_ROWS, n_rows + 1, BF16_ROWS)
                if n_rows % r == 0 and n_rows // r <= n_steps)
    last = n_rows // rows - 1
    return pl.BlockSpec((rows, n_cols), lambda b, s: (jnp.minimum(b * n_seq + s, last), 0))


def _mixer_call(layer, x, mem, g1, w_in, lng, lnb, wsp, bsp, lbl, hng, mg, wkv, wbr, wout,
                wup_f32, wdn_f32):
    bsz, seq, d = x.shape
    mlen = mem.shape[1]
    ts = SEQ_TILE
    n_seq = seq // ts
    consts = (g1, w_in, lng, lnb, wsp, bsp, lbl, hng, mg, wkv, wbr, wout)
    cast_specs = [_cast_block_spec(*w.shape, n_seq, bsz * n_seq) for w in (wup_f32, wdn_f32)]
    return pl.pallas_call(
        functools.partial(_mixer_kernel, layer),
        grid=(bsz, n_seq),
        in_specs=[pl.BlockSpec((1, ts, d), lambda b, s: (b, s, 0)),
                  pl.BlockSpec((1, mlen, d), lambda b, s: (b, 0, 0))]
                 + [_const_spec(c.shape) for c in consts] + cast_specs,
        out_specs=[pl.BlockSpec((1, ts, d), lambda b, s: (b, s, 0))] + cast_specs,
        out_shape=[jax.ShapeDtypeStruct(x.shape, F32),
                   jax.ShapeDtypeStruct(wup_f32.shape, BF16),
                   jax.ShapeDtypeStruct(wdn_f32.shape, BF16)],
        scratch_shapes=[pltpu.VMEM((XA_WIDTH, mlen), BF16),
                        pltpu.VMEM((mlen, XA_WIDTH), BF16),
                        pltpu.VMEM((HG_HEADS, HG_DIM, HG_DIM), F32),
                        pltpu.VMEM((N_BRANCH, ts, BRANCH_WIDTH), BF16),
                        pltpu.VMEM((ts, HG_WIDTH), F32)],
        compiler_params=pltpu.CompilerParams(
            dimension_semantics=("arbitrary", "arbitrary"),
            vmem_limit_bytes=V7X_VMEM_LIMIT_BYTES),
        name="mixer",
    )(x, mem, *consts, wup_f32, wdn_f32)


def _ffn_call(final, x, g2, wup, cw, cb, wdn, gf):
    bsz, seq, d = x.shape
    dff = wdn.shape[0]
    ts = SEQ_TILE
    consts = (g2, wup, cw, cb, wdn, gf)
    return pl.pallas_call(
        functools.partial(_ffn_kernel, final),
        grid=(bsz, seq // ts),
        in_specs=[pl.BlockSpec((1, ts, d), lambda b, s: (b, s, 0))]
                 + [_const_spec(c.shape) for c in consts],
        out_specs=pl.BlockSpec((1, ts, d), lambda b, s: (b, s, 0)),
        out_shape=jax.ShapeDtypeStruct(x.shape, F32),
        scratch_shapes=[pltpu.VMEM((8, dff), F32)],
        compiler_params=pltpu.CompilerParams(
            dimension_semantics=("parallel", "arbitrary"),
            vmem_limit_bytes=V7X_VMEM_LIMIT_BYTES),
        name="ffn",
    )(x, *consts)


def kernel(x, mem, norm1_g, w_in, ln_v_g, ln_v_b, w_spatial, b_spatial, lb_logits, hgrn_norm_g,
           mem_norm_g, w_mem_kv, w_branch, w_out, norm2_g, w_up, conv_w, conv_b, w_down, final_g):
    depth = w_in.shape[0]
    assert x.shape[1] % SEQ_TILE == 0 and SEQ_TILE % ROW_GROUP == 0
    assert ROW_GROUP % GM_CHUNK == 0 and ROW_GROUP % HG_CHUNK == 0
    assert HG_CHUNK % HG_SUB == 0 and HG_SUB % 16 == 0
    row = lambda p: p.reshape(1, -1)
    gd = GM_WIDTH // GM_GROUPS
    d = x.shape[-1]
    sections = ((GM_WIDTH, 0.5), (GM_WIDTH, 0.5), (HG_WIDTH, 0.5), (HG_WIDTH, 0.5), (HG_WIDTH, 1.0),
                (HG_WIDTH, 0.5), (XA_WIDTH, 1.0), (N_BRANCH * d, 0.5))
    in_scale = jnp.concatenate([jnp.full((1, w), s, F32) for w, s in sections], axis=1)
    for l in range(depth):
        bsp = jnp.repeat(b_spatial[l].T, gd, axis=1)
        x, wup, wdn = _mixer_call(
            l, x, mem, row(norm1_g[l]), (w_in[l] * in_scale).astype(BF16),
            row(ln_v_g[l]), row(ln_v_b[l]), w_spatial[l].astype(BF16), bsp, lb_logits,
            row(hgrn_norm_g[l]), row(mem_norm_g[l]), w_mem_kv[l].astype(BF16),
            (0.5 * w_branch[l]).astype(BF16), w_out[l].astype(BF16), w_up[l], w_down[l])
        x = _ffn_call(l == depth - 1, x, row(norm2_g[l]), wup, 0.5 * conv_w[l],
                      row(0.5 * conv_b[l]), wdn, row(final_g))
    return x
```

```python
import functools

import jax
import jax.numpy as jnp
from jax import lax
from jax.experimental import pallas as pl
from jax.experimental.pallas import tpu as pltpu

F32 = jnp.float32
BF16 = jnp.bfloat16

EPS = 1e-6
GM_GROUPS = 4
GM_CHUNK = 128
GM_WIDTH = 512
HG_HEADS = 4
HG_DIM = 128
HG_WIDTH = HG_HEADS * HG_DIM
HG_CHUNK = 64
HG_SUB = 32
XA_HEADS = 4
XA_DIM = 128
XA_WIDTH = XA_HEADS * XA_DIM
N_BRANCH = 3
BRANCH_WIDTH = 512

ROW_GROUP = 256
SEQ_TILE = 1024
V7X_VMEM_LIMIT_BYTES = 56 * 1024 * 1024
V7X_MXU_DEPTH = 256
BF16_ROWS = 16
F32_ROWS = 8


def _dot(a, b):
    return jnp.dot(a, b, preferred_element_type=F32)


def _dot_nt(a, b):
    return lax.dot_general(a, b, (((1,), (1,)), ((), ())), preferred_element_type=F32)


def _dot_tn(a, b):
    return lax.dot_general(a, b, (((0,), (0,)), ((), ())), preferred_element_type=F32)


def _silu_half(h):
    return h * jnp.tanh(h) + h


GELU_K0 = 2.0 * 0.7978845608028654
GELU_K1 = 8.0 * 0.044715 * 0.7978845608028654


def _gelu_half(h):
    return h * jnp.tanh(h * (GELU_K0 + GELU_K1 * (h * h))) + h


def _rms(x):
    return x * lax.rsqrt(jnp.mean(x * x, axis=-1, keepdims=True) + EPS)


def _mixer_kernel(layer, x_ref, mem_ref, g1_ref, w_in_ref, lng_ref, lnb_ref, wsp_ref, bsp_ref,
                  lbl_ref, hng_ref, mg_ref, wkv_ref, wbr_ref, wout_ref, wup_f32_ref, wdn_f32_ref,
                  o_ref, wup_bf16_ref, wdn_bf16_ref, kt_sc, v_sc, st_sc, br_sc, ho_sc):
    ts = x_ref.shape[1]
    s_idx = pl.program_id(1)

    wup_bf16_ref[...] = wup_f32_ref[...].astype(BF16)
    wdn_bf16_ref[...] = wdn_f32_ref[...].astype(BF16)

    @pl.when(s_idx == 0)
    def _start_of_sequence():
        m = mem_ref[0]
        mn = (_rms(m) * mg_ref[...]).astype(BF16)
        kv = _dot(mn, wkv_ref[...])
        kt_sc[...] = kv[:, :XA_WIDTH].T.astype(BF16)
        v_sc[...] = kv[:, XA_WIDTH:].astype(BF16)
        st_sc[...] = jnp.zeros_like(st_sc)

    st = [st_sc[h] for h in range(HG_HEADS)]
    for r0 in range(0, ts, ROW_GROUP):
        st = _mixer_rows(layer, r0, st, x_ref, g1_ref, w_in_ref, lng_ref, lnb_ref, wsp_ref, bsp_ref,
                         lbl_ref, hng_ref, wbr_ref, wout_ref, o_ref, kt_sc, v_sc, br_sc, ho_sc)
    for h in range(HG_HEADS):
        st_sc[h] = st[h]


def _mixer_rows(layer, r0, st, x_ref, g1_ref, w_in_ref, lng_ref, lnb_ref, wsp_ref, bsp_ref,
                lbl_ref, hng_ref, wbr_ref, wout_ref, o_ref, kt_sc, v_sc, br_sc, ho_sc):
    ts = ROW_GROUP
    grp = slice(r0, r0 + ts)
    x = x_ref[0, grp]
    hb = (_rms(x) * g1_ref[...]).astype(BF16)
    d = x.shape[-1]

    def proj(lo, width):
        return _dot(hb, w_in_ref[:, lo:lo + width])

    c_u, c_v = 0, GM_WIDTH
    c_q = 2 * GM_WIDTH
    c_f, c_i, c_g = c_q + HG_WIDTH, c_q + 2 * HG_WIDTH, c_q + 3 * HG_WIDTH
    c_x = c_q + 4 * HG_WIDTH
    c_gate = c_x + XA_WIDTH

    p_u = proj(c_u, GM_WIDTH)
    p_v = proj(c_v, GM_WIDTH)
    hq = proj(c_q, HG_WIDTH)
    hf = proj(c_f, HG_WIDTH)
    hi = proj(c_i, HG_WIDTH)
    hg = proj(c_g, HG_WIDTH)

    u = _gelu_half(p_u)
    v = _gelu_half(p_v)
    mu = jnp.mean(v, axis=-1, keepdims=True)
    vc = v - mu
    var = jnp.mean(vc * vc, axis=-1, keepdims=True)
    vn = ((vc * lax.rsqrt(var + EPS)) * lng_ref[...] + lnb_ref[...]).astype(BF16)
    p_x = proj(c_x, XA_WIDTH)
    r128 = lax.broadcasted_iota(jnp.int32, (GM_CHUNK, GM_CHUNK), 0)
    c128 = lax.broadcasted_iota(jnp.int32, (GM_CHUNK, GM_CHUNK), 1)
    tril128 = c128 <= r128
    gd = GM_WIDTH // GM_GROUPS
    for g in range(GM_GROUPS):
        w_g = jnp.where(tril128, wsp_ref[g], jnp.zeros((), BF16))
        for c in range(ts // GM_CHUNK):
            rows = slice(c * GM_CHUNK, (c + 1) * GM_CHUNK)
            cols = slice(g * gd, (g + 1) * gd)
            mixed = _dot(w_g, vn[rows, cols]) + bsp_ref[:, cols]
            br_sc[0, rows, cols] = (
                u[rows, cols] * mixed).astype(BF16)
    gl = [proj(c_gate, d)]

    ll = lbl_ref[...]
    le = jnp.exp(ll - jnp.max(ll, axis=0, keepdims=True))
    lb = jnp.sum(le[:layer + 1], axis=0, keepdims=True) / jnp.sum(le, axis=0, keepdims=True)
    fg = (0.5 + 0.5 * lb) + (0.5 - 0.5 * lb) * jnp.tanh(hf)
    kk = 1.0 - fg
    logf = jnp.log(fg)
    qf = _silu_half(hq)
    cr = min(ts, V7X_MXU_DEPTH)
    rt = lax.broadcasted_iota(jnp.int32, (cr, cr), 0)
    ct = lax.broadcasted_iota(jnp.int32, (cr, cr), 1)
    cum_m = jnp.where((ct <= rt) & (rt // HG_SUB == ct // HG_SUB), 1.0, 0.0).astype(BF16)
    lf_hi = logf.astype(BF16)
    lf_lo = (logf - lf_hi.astype(F32)).astype(BF16)
    a_sub = jnp.concatenate(
        [_dot(cum_m, lf_hi[r:r + cr]) + _dot(cum_m, lf_lo[r:r + cr]) for r in range(0, ts, cr)],
        axis=0)

    xq = (p_x * (XA_DIM ** -0.5)).astype(BF16)
    att_s = [_dot(xq[:, h * XA_DIM:(h + 1) * XA_DIM], kt_sc[h * XA_DIM:(h + 1) * XA_DIM, :])
             for h in range(XA_HEADS)]
    gl.append(proj(c_gate + d, d))

    rc = lax.broadcasted_iota(jnp.int32, (HG_CHUNK, HG_CHUNK), 0)
    cc = lax.broadcasted_iota(jnp.int32, (HG_CHUNK, HG_CHUNK), 1)
    mask_same = (cc <= rc) & (rc // HG_SUB == cc // HG_SUB)
    mask_later = rc // HG_SUB > cc // HG_SUB
    vb = hi.astype(BF16)
    nsub = HG_CHUNK // HG_SUB
    chunks = []
    for c in range(ts // HG_CHUNK):
        a_blk = [a_sub[c * HG_CHUNK + j * HG_SUB:c * HG_CHUNK + (j + 1) * HG_SUB] for j in range(nsub)]
        tot = [a_j[HG_SUB - 1:HG_SUB] for a_j in a_blk]
        qd, kd, q1, k1, qi, ke = [], [], [], [], [], []
        for j in range(nsub):
            b0 = c * HG_CHUNK + j * HG_SUB
            q_j, k_j, a_j = qf[b0:b0 + HG_SUB], kk[b0:b0 + HG_SUB], a_blk[j]
            z = a_j - a_j[HG_SUB // 2 - 1:HG_SUB // 2]
            qd.append(q_j * jnp.exp(z))
            kd.append(k_j * jnp.exp(-z))
            q1_j = q_j * jnp.exp(a_j)
            k1_j = k_j * jnp.exp(tot[j] - a_j)
            q1.append(q1_j)
            k1.append(k1_j)
            before = sum(tot[:j]) if j else None
            after = sum(tot[j + 1:]) if j + 1 < nsub else None
            qi.append(q1_j if before is None else q1_j * jnp.exp(before))
            ke.append(k1_j if after is None else k1_j * jnp.exp(after))
        cat = lambda parts: jnp.concatenate(parts, axis=0).astype(BF16)
        chunks.append((cat(qd), cat(kd), cat(q1), cat(k1), cat(qi), cat(ke), sum(tot)))
    pre = {}
    for c, (qd, kd, q1, k1, qi, ke, _) in enumerate(chunks):
        rows = slice(c * HG_CHUNK, (c + 1) * HG_CHUNK)
        for h in range(HG_HEADS):
            cols = slice(h * HG_DIM, (h + 1) * HG_DIM)
            pre[c, h] = (_dot_nt(qd[:, cols], kd[:, cols]), _dot_nt(q1[:, cols], k1[:, cols]),
                         _dot_tn(ke[:, cols], vb[rows, cols]))

    for h in range(XA_HEADS):
        cols = slice(h * XA_DIM, (h + 1) * XA_DIM)
        s = att_s[h]
        e = jnp.exp(s - jnp.max(s, axis=-1, keepdims=True))
        inv = 1.0 / jnp.sum(e, axis=-1, keepdims=True)
        br_sc[2, :, cols] = (_dot(e.astype(BF16), v_sc[:, cols]) * inv).astype(BF16)
    gl.append(proj(c_gate + 2 * d, d))

    st = list(st)
    log_dec = [ch[6] for ch in chunks]
    log_dec += [jnp.zeros_like(log_dec[0])] * (-len(log_dec) % F32_ROWS)
    dec_t = jnp.exp(jnp.concatenate(log_dec, axis=0)).T
    for c, (qd, kd, q1, k1, qi, ke, _) in enumerate(chunks):
        rows = slice(c * HG_CHUNK, (c + 1) * HG_CHUNK)
        for h in range(HG_HEADS):
            cols = slice(h * HG_DIM, (h + 1) * HG_DIM)
            s_same, s_later, upd = pre[c, h]
            sc = jnp.where(mask_same, s_same, jnp.where(mask_later, s_later, 0.0))
            o = _dot(sc.astype(BF16), vb[rows, cols]) + _dot(qi[:, cols], st[h].astype(BF16))
            ho_sc[rows, cols] = o
            st[h] = st[h] * dec_t[cols, c:c + 1] + upd
    hg_act = _silu_half(hg)
    for h in range(HG_HEADS):
        cols = slice(h * HG_DIM, (h + 1) * HG_DIM)
        o = ho_sc[:, cols]
        on = _rms(o) * hng_ref[...]
        br_sc[1, :, cols] = (on * hg_act[:, cols]).astype(BF16)

    merged = None
    for n in range(N_BRANCH):
        up = _dot(br_sc[n], wbr_ref[n])
        term = jnp.tanh(gl[n]) * up + up
        merged = term if merged is None else merged + term
    o_ref[0, grp] = x + _dot(merged.astype(BF16), wout_ref[...])
    return st


def _ffn_kernel(final, x_ref, g2_ref, wup_ref, cw_ref, cb_ref, wdn_ref, gf_ref, o_ref, tail_sc):
    ts = x_ref.shape[1]
    dff = wdn_ref.shape[0]
    s_idx = pl.program_id(1)

    @pl.when(s_idx == 0)
    def _start_of_sequence():
        tail_sc[...] = jnp.zeros_like(tail_sc)

    tail = tail_sc[...]
    rg = ROW_GROUP
    row = lax.broadcasted_iota(jnp.int32, (rg, 1), 0)
    for r0 in range(0, ts, rg):
        x = x_ref[0, r0:r0 + rg]
        hb = (_rms(x) * g2_ref[...]).astype(BF16)
        a = _dot(hb, wup_ref[:, :dff])
        b = _dot(hb, wup_ref[:, dff:])
        prev1 = tail[F32_ROWS - 1:F32_ROWS, :]
        prev2 = tail[F32_ROWS - 2:F32_ROWS - 1, :]
        a1 = jnp.where(row == 0, prev1, pltpu.roll(a, 1, axis=0))
        a2 = jnp.where(row == 0, prev2, jnp.where(row == 1, prev1, pltpu.roll(a, 2, axis=0)))
        tail = a[rg - F32_ROWS:, :]
        ac = cb_ref[...] + cw_ref[0:1, :] * a2 + cw_ref[1:2, :] * a1 + cw_ref[2:3, :] * a
        y = (_silu_half(ac) * b).astype(BF16)
        out = x + _dot(y, wdn_ref[...])
        if final:
            out = _rms(out) * gf_ref[...]
        o_ref[0, r0:r0 + rg] = out
    tail_sc[...] = tail


def _const_spec(shape):
    return pl.BlockSpec(shape, lambda b, s: (0,) * len(shape), pipeline_mode=pl.Buffered(1))


def _cast_block_spec(n_rows, n_cols, n_seq, n_steps):
    rows = next(r for r in range(BF16_ROWS, n_rows + 1, BF16_ROWS)
                if n_rows % r == 0 and n_rows // r <= n_steps)
    last = n_rows // rows - 1
    return pl.BlockSpec((rows, n_cols), lambda b, s: (jnp.minimum(b * n_seq + s, last), 0))


def _mixer_call(layer, x, mem, g1, w_in, lng, lnb, wsp, bsp, lbl, hng, mg, wkv, wbr, wout,
                wup_f32, wdn_f32):
    bsz, seq, d = x.shape
    mlen = mem.shape[1]
    ts = SEQ_TILE
    n_seq = seq // ts
    consts = (g1, w_in, lng, lnb, wsp, bsp, lbl, hng, mg, wkv, wbr, wout)
    cast_specs = [_cast_block_spec(*w.shape, n_seq, bsz * n_seq) for w in (wup_f32, wdn_f32)]
    return pl.pallas_call(
        functools.partial(_mixer_kernel, layer),
        grid=(bsz, n_seq),
        in_specs=[pl.BlockSpec((1, ts, d), lambda b, s: (b, s, 0)),
                  pl.BlockSpec((1, mlen, d), lambda b, s: (b, 0, 0))]
                 + [_const_spec(c.shape) for c in consts] + cast_specs,
        out_specs=[pl.BlockSpec((1, ts, d), lambda b, s: (b, s, 0))] + cast_specs,
        out_shape=[jax.ShapeDtypeStruct(x.shape, F32),
                   jax.ShapeDtypeStruct(wup_f32.shape, BF16),
                   jax.ShapeDtypeStruct(wdn_f32.shape, BF16)],
        scratch_shapes=[pltpu.VMEM((XA_WIDTH, mlen), BF16),
                        pltpu.VMEM((mlen, XA_WIDTH), BF16),
                        pltpu.VMEM((HG_HEADS, HG_DIM, HG_DIM), F32),
                        pltpu.VMEM((N_BRANCH, ROW_GROUP, BRANCH_WIDTH), BF16),
                        pltpu.VMEM((ROW_GROUP, HG_WIDTH), F32)],
        compiler_params=pltpu.CompilerParams(
            dimension_semantics=("arbitrary", "arbitrary"),
            vmem_limit_bytes=V7X_VMEM_LIMIT_BYTES),
        name="mixer",
    )(x, mem, *consts, wup_f32, wdn_f32)


def _ffn_call(final, x, g2, wup, cw, cb, wdn, gf):
    bsz, seq, d = x.shape
    dff = wdn.shape[0]
    ts = SEQ_TILE
    consts = (g2, wup, cw, cb, wdn, gf)
    return pl.pallas_call(
        functools.partial(_ffn_kernel, final),
        grid=(bsz, seq // ts),
        in_specs=[pl.BlockSpec((1, ts, d), lambda b, s: (b, s, 0))]
                 + [_const_spec(c.shape) for c in consts],
        out_specs=pl.BlockSpec((1, ts, d), lambda b, s: (b, s, 0)),
        out_shape=jax.ShapeDtypeStruct(x.shape, F32),
        scratch_shapes=[pltpu.VMEM((F32_ROWS, dff), F32)],
        compiler_params=pltpu.CompilerParams(
            dimension_semantics=("parallel", "arbitrary"),
            vmem_limit_bytes=V7X_VMEM_LIMIT_BYTES),
        name="ffn",
    )(x, *consts)


def kernel(x, mem, norm1_g, w_in, ln_v_g, ln_v_b, w_spatial, b_spatial, lb_logits, hgrn_norm_g,
           mem_norm_g, w_mem_kv, w_branch, w_out, norm2_g, w_up, conv_w, conv_b, w_down, final_g):
    depth = w_in.shape[0]
    assert x.shape[1] % SEQ_TILE == 0 and SEQ_TILE % ROW_GROUP == 0
    assert ROW_GROUP % GM_CHUNK == 0 and ROW_GROUP % HG_CHUNK == 0
    assert HG_CHUNK % HG_SUB == 0 and HG_SUB % 16 == 0
    row = lambda p: p.reshape(1, -1)
    gd = GM_WIDTH // GM_GROUPS
    d = x.shape[-1]
    sections = ((GM_WIDTH, 0.5), (GM_WIDTH, 0.5), (HG_WIDTH, 0.5), (HG_WIDTH, 0.5), (HG_WIDTH, 1.0),
                (HG_WIDTH, 0.5), (XA_WIDTH, 1.0), (N_BRANCH * d, 0.5))
    in_scale = jnp.concatenate([jnp.full((1, w), s, F32) for w, s in sections], axis=1)
    for l in range(depth):
        bsp = jnp.repeat(b_spatial[l].T, gd, axis=1)
        x, wup, wdn = _mixer_call(
            l, x, mem, row(norm1_g[l]), (w_in[l] * in_scale).astype(BF16),
            row(ln_v_g[l]), row(ln_v_b[l]), w_spatial[l].astype(BF16), bsp, lb_logits,
            row(hgrn_norm_g[l]), row(mem_norm_g[l]), w_mem_kv[l].astype(BF16),
            (0.5 * w_branch[l]).astype(BF16), w_out[l].astype(BF16), w_up[l], w_down[l])
        x = _ffn_call(l == depth - 1, x, row(norm2_g[l]), wup, 0.5 * conv_w[l],
                      row(0.5 * conv_b[l]), wdn, row(final_g))
    return x
```

```python
import functools

import jax
import jax.numpy as jnp
from jax import lax
from jax.experimental import pallas as pl
from jax.experimental.pallas import tpu as pltpu

F32 = jnp.float32
BF16 = jnp.bfloat16

EPS = 1e-6
GM_GROUPS = 4
GM_CHUNK = 128
GM_WIDTH = 512
HG_HEADS = 4
HG_DIM = 128
HG_WIDTH = HG_HEADS * HG_DIM
HG_CHUNK = 64
HG_SUB = 32
XA_HEADS = 4
XA_DIM = 128
XA_WIDTH = XA_HEADS * XA_DIM
N_BRANCH = 3
BRANCH_WIDTH = 512

P_NORM1_G, P_MEM_G, P_LN, P_HG_G, P_LB = 0, 1, 2, 3, 4
ROW_GROUP = 256
SEQ_TILE = 1024
V7X_VMEM_LIMIT_BYTES = 56 * 1024 * 1024
V7X_MXU_DEPTH = 256
BF16_ROWS = 16
F32_ROWS = 8


def _dot(a, b):
    return jnp.dot(a, b, preferred_element_type=F32)


def _dot_nt(a, b):
    return lax.dot_general(a, b, (((1,), (1,)), ((), ())), preferred_element_type=F32)


def _dot_tn(a, b):
    return lax.dot_general(a, b, (((0,), (0,)), ((), ())), preferred_element_type=F32)


def _silu_half(h):
    return h * jnp.tanh(h) + h


GELU_K0 = 2.0 * 0.7978845608028654
GELU_K1 = 8.0 * 0.044715 * 0.7978845608028654


def _gelu_half(h):
    return h * jnp.tanh(h * (GELU_K0 + GELU_K1 * (h * h))) + h


def _rms(x):
    return x * lax.rsqrt(jnp.mean(x * x, axis=-1, keepdims=True) + EPS)


def _memkv_kernel(mem_ref, p_ref, wkv_ref, kt_ref, v_ref):
    mn = (_rms(mem_ref[0]) * p_ref[P_MEM_G:P_MEM_G + 1, :]).astype(BF16)
    kv = _dot(mn, wkv_ref[...])
    kt_ref[0] = kv[:, :XA_WIDTH].T.astype(BF16)
    v_ref[0] = kv[:, XA_WIDTH:].astype(BF16)


def _mixer_kernel(layer, n_layers, x_ref, kt_ref, v_ref, p_ref, w_in_ref, wsp_ref, bsp_ref,
                  wbr_ref, wout_ref, wup_f32_ref, wdn_f32_ref,
                  o_ref, wup_bf16_ref, wdn_bf16_ref, st_sc, br_sc, ho_sc):
    ts = x_ref.shape[1]
    s_idx = pl.program_id(1)
    kt_sc, v_sc = kt_ref.at[0], v_ref.at[0]

    wup_bf16_ref[...] = wup_f32_ref[...].astype(BF16)
    wdn_bf16_ref[...] = wdn_f32_ref[...].astype(BF16)

    @pl.when(s_idx == 0)
    def _start_of_sequence():
        st_sc[...] = jnp.zeros_like(st_sc)

    st = [st_sc[h] for h in range(HG_HEADS)]
    for r0 in range(0, ts, ROW_GROUP):
        st = _mixer_rows(layer, n_layers, r0, st, x_ref, p_ref, w_in_ref, wsp_ref, bsp_ref,
                         wbr_ref, wout_ref, o_ref, kt_sc, v_sc, br_sc, ho_sc)
    for h in range(HG_HEADS):
        st_sc[h] = st[h]


def _mixer_rows(layer, n_layers, r0, st, x_ref, p_ref, w_in_ref, wsp_ref, bsp_ref,
                wbr_ref, wout_ref, o_ref, kt_sc, v_sc, br_sc, ho_sc):
    ts = ROW_GROUP
    grp = slice(r0, r0 + ts)
    x = x_ref[0, grp]
    hb = (_rms(x) * p_ref[P_NORM1_G:P_NORM1_G + 1, :]).astype(BF16)
    d = x.shape[-1]

    def proj(lo, width):
        return _dot(hb, w_in_ref[:, lo:lo + width])

    c_u, c_v = 0, GM_WIDTH
    c_q = 2 * GM_WIDTH
    c_f, c_i, c_g = c_q + HG_WIDTH, c_q + 2 * HG_WIDTH, c_q + 3 * HG_WIDTH
    c_x = c_q + 4 * HG_WIDTH
    c_gate = c_x + XA_WIDTH

    p_u = proj(c_u, GM_WIDTH)
    p_v = proj(c_v, GM_WIDTH)
    hq = proj(c_q, HG_WIDTH)
    hf = proj(c_f, HG_WIDTH)
    hi = proj(c_i, HG_WIDTH)
    hg = proj(c_g, HG_WIDTH)

    u = _gelu_half(p_u)
    v = _gelu_half(p_v)
    mu = jnp.mean(v, axis=-1, keepdims=True)
    vc = v - mu
    var = jnp.mean(vc * vc, axis=-1, keepdims=True)
    ln = p_ref[P_LN:P_LN + 1, :]
    vn = ((vc * lax.rsqrt(var + EPS)) * ln[:, :GM_WIDTH] + ln[:, GM_WIDTH:]).astype(BF16)
    p_x = proj(c_x, XA_WIDTH)
    r128 = lax.broadcasted_iota(jnp.int32, (GM_CHUNK, GM_CHUNK), 0)
    c128 = lax.broadcasted_iota(jnp.int32, (GM_CHUNK, GM_CHUNK), 1)
    tril128 = c128 <= r128
    gd = GM_WIDTH // GM_GROUPS
    for g in range(GM_GROUPS):
        w_g = jnp.where(tril128, wsp_ref[g], jnp.zeros((), BF16))
        for c in range(ts // GM_CHUNK):
            rows = slice(c * GM_CHUNK, (c + 1) * GM_CHUNK)
            cols = slice(g * gd, (g + 1) * gd)
            mixed = _dot(w_g, vn[rows, cols]) + bsp_ref[:, cols]
            br_sc[0, rows, cols] = (
                u[rows, cols] * mixed).astype(BF16)
    gl = [proj(c_gate, d)]

    ll = p_ref[P_LB:P_LB + n_layers + 1, :HG_WIDTH]
    le = jnp.exp(ll - jnp.max(ll, axis=0, keepdims=True))
    lb = jnp.sum(le[:layer + 1], axis=0, keepdims=True) / jnp.sum(le, axis=0, keepdims=True)
    fg = (0.5 + 0.5 * lb) + (0.5 - 0.5 * lb) * jnp.tanh(hf)
    kk = 1.0 - fg
    logf = jnp.log(fg)
    qf = _silu_half(hq)
    cr = min(ts, V7X_MXU_DEPTH)
    rt = lax.broadcasted_iota(jnp.int32, (cr, cr), 0)
    ct = lax.broadcasted_iota(jnp.int32, (cr, cr), 1)
    cum_m = jnp.where((ct <= rt) & (rt // HG_SUB == ct // HG_SUB), 1.0, 0.0).astype(BF16)
    lf_hi = logf.astype(BF16)
    lf_lo = (logf - lf_hi.astype(F32)).astype(BF16)
    a_sub = jnp.concatenate(
        [_dot(cum_m, lf_hi[r:r + cr]) + _dot(cum_m, lf_lo[r:r + cr]) for r in range(0, ts, cr)],
        axis=0)

    xq = (p_x * (XA_DIM ** -0.5)).astype(BF16)
    att_s = [_dot(xq[:, h * XA_DIM:(h + 1) * XA_DIM], kt_sc[h * XA_DIM:(h + 1) * XA_DIM, :])
             for h in range(XA_HEADS)]
    gl.append(proj(c_gate + d, d))

    rc = lax.broadcasted_iota(jnp.int32, (HG_CHUNK, HG_CHUNK), 0)
    cc = lax.broadcasted_iota(jnp.int32, (HG_CHUNK, HG_CHUNK), 1)
    mask_same = (cc <= rc) & (rc // HG_SUB == cc // HG_SUB)
    mask_later = rc // HG_SUB > cc // HG_SUB
    vb = hi.astype(BF16)
    nsub = HG_CHUNK // HG_SUB
    chunks = []
    for c in range(ts // HG_CHUNK):
        a_blk = [a_sub[c * HG_CHUNK + j * HG_SUB:c * HG_CHUNK + (j + 1) * HG_SUB] for j in range(nsub)]
        tot = [a_j[HG_SUB - 1:HG_SUB] for a_j in a_blk]
        qd, kd, q1, k1, qi, ke = [], [], [], [], [], []
        for j in range(nsub):
            b0 = c * HG_CHUNK + j * HG_SUB
            q_j, k_j, a_j = qf[b0:b0 + HG_SUB], kk[b0:b0 + HG_SUB], a_blk[j]
            z = a_j - a_j[HG_SUB // 2 - 1:HG_SUB // 2]
            qd.append(q_j * jnp.exp(z))
            kd.append(k_j * jnp.exp(-z))
            q1_j = q_j * jnp.exp(a_j)
            k1_j = k_j * jnp.exp(tot[j] - a_j)
            q1.append(q1_j)
            k1.append(k1_j)
            before = sum(tot[:j]) if j else None
            after = sum(tot[j + 1:]) if j + 1 < nsub else None
            qi.append(q1_j if before is None else q1_j * jnp.exp(before))
            ke.append(k1_j if after is None else k1_j * jnp.exp(after))
        cat = lambda parts: jnp.concatenate(parts, axis=0).astype(BF16)
        chunks.append((cat(qd), cat(kd), cat(q1), cat(k1), cat(qi), cat(ke), sum(tot)))
    pre = {}
    for c, (qd, kd, q1, k1, qi, ke, _) in enumerate(chunks):
        rows = slice(c * HG_CHUNK, (c + 1) * HG_CHUNK)
        for h in range(HG_HEADS):
            cols = slice(h * HG_DIM, (h + 1) * HG_DIM)
            pre[c, h] = (_dot_nt(qd[:, cols], kd[:, cols]), _dot_nt(q1[:, cols], k1[:, cols]),
                         _dot_tn(ke[:, cols], vb[rows, cols]))

    for h in range(XA_HEADS):
        cols = slice(h * XA_DIM, (h + 1) * XA_DIM)
        s = att_s[h]
        e = jnp.exp(s - jnp.max(s, axis=-1, keepdims=True))
        inv = 1.0 / jnp.sum(e, axis=-1, keepdims=True)
        br_sc[2, :, cols] = (_dot(e.astype(BF16), v_sc[:, cols]) * inv).astype(BF16)
    gl.append(proj(c_gate + 2 * d, d))

    st = list(st)
    log_dec = [ch[6] for ch in chunks]
    log_dec += [jnp.zeros_like(log_dec[0])] * (-len(log_dec) % F32_ROWS)
    dec_t = jnp.exp(jnp.concatenate(log_dec, axis=0)).T
    for c, (qd, kd, q1, k1, qi, ke, _) in enumerate(chunks):
        rows = slice(c * HG_CHUNK, (c + 1) * HG_CHUNK)
        for h in range(HG_HEADS):
            cols = slice(h * HG_DIM, (h + 1) * HG_DIM)
            s_same, s_later, upd = pre[c, h]
            sc = jnp.where(mask_same, s_same, jnp.where(mask_later, s_later, 0.0))
            o = _dot(sc.astype(BF16), vb[rows, cols]) + _dot(qi[:, cols], st[h].astype(BF16))
            ho_sc[rows, cols] = o
            st[h] = st[h] * dec_t[cols, c:c + 1] + upd
    hg_act = _silu_half(hg)
    for h in range(HG_HEADS):
        cols = slice(h * HG_DIM, (h + 1) * HG_DIM)
        o = ho_sc[:, cols]
        on = _rms(o) * p_ref[P_HG_G:P_HG_G + 1, :HG_DIM]
        br_sc[1, :, cols] = (on * hg_act[:, cols]).astype(BF16)

    merged = None
    for n in range(N_BRANCH):
        up = _dot(br_sc[n], wbr_ref[n])
        term = jnp.tanh(gl[n]) * up + up
        merged = term if merged is None else merged + term
    o_ref[0, grp] = x + _dot(merged.astype(BF16), wout_ref[...])
    return st


def _ffn_kernel(final, x_ref, g2_ref, wup_ref, cw_ref, cb_ref, wdn_ref, gf_ref, o_ref, tail_sc):
    ts = x_ref.shape[1]
    dff = wdn_ref.shape[0]
    s_idx = pl.program_id(1)

    @pl.when(s_idx == 0)
    def _start_of_sequence():
        tail_sc[...] = jnp.zeros_like(tail_sc)

    tail = tail_sc[...]
    rg = ROW_GROUP
    row = lax.broadcasted_iota(jnp.int32, (rg, 1), 0)
    for r0 in range(0, ts, rg):
        x = x_ref[0, r0:r0 + rg]
        hb = (_rms(x) * g2_ref[...]).astype(BF16)
        a = _dot(hb, wup_ref[:, :dff])
        b = _dot(hb, wup_ref[:, dff:])
        prev1 = tail[F32_ROWS - 1:F32_ROWS, :]
        prev2 = tail[F32_ROWS - 2:F32_ROWS - 1, :]
        a1 = jnp.where(row == 0, prev1, pltpu.roll(a, 1, axis=0))
        a2 = jnp.where(row == 0, prev2, jnp.where(row == 1, prev1, pltpu.roll(a, 2, axis=0)))
        tail = a[rg - F32_ROWS:, :]
        ac = cb_ref[...] + cw_ref[0:1, :] * a2 + cw_ref[1:2, :] * a1 + cw_ref[2:3, :] * a
        y = (_silu_half(ac) * b).astype(BF16)
        out = x + _dot(y, wdn_ref[...])
        if final:
            out = _rms(out) * gf_ref[...]
        o_ref[0, r0:r0 + rg] = out
    tail_sc[...] = tail


def _const_spec(shape):
    return pl.BlockSpec(shape, lambda *_: (0,) * len(shape), pipeline_mode=pl.Buffered(1))


def _cast_block_spec(n_rows, n_cols, n_seq, n_steps):
    rows = next(r for r in range(BF16_ROWS, n_rows + 1, BF16_ROWS)
                if n_rows % r == 0 and n_rows // r <= n_steps)
    last = n_rows // rows - 1
    return pl.BlockSpec((rows, n_cols), lambda b, s: (jnp.minimum(b * n_seq + s, last), 0))


def _memkv_call(mem, params, wkv):
    bsz, mlen, d = mem.shape
    return pl.pallas_call(
        _memkv_kernel,
        grid=(bsz,),
        in_specs=[pl.BlockSpec((1, mlen, d), lambda b: (b, 0, 0)),
                  _const_spec(params.shape), _const_spec(wkv.shape)],
        out_specs=[pl.BlockSpec((1, XA_WIDTH, mlen), lambda b: (b, 0, 0)),
                   pl.BlockSpec((1, mlen, XA_WIDTH), lambda b: (b, 0, 0))],
        out_shape=[jax.ShapeDtypeStruct((bsz, XA_WIDTH, mlen), BF16),
                   jax.ShapeDtypeStruct((bsz, mlen, XA_WIDTH), BF16)],
        compiler_params=pltpu.CompilerParams(dimension_semantics=("parallel",)),
        name="memkv",
    )(mem, params, wkv)


def _mixer_call(layer, n_layers, x, kt, v, params, w_in, wsp, bsp, wbr, wout, wup_f32, wdn_f32):
    bsz, seq, d = x.shape
    mlen = v.shape[1]
    ts = SEQ_TILE
    n_seq = seq // ts
    consts = (params, w_in, wsp, bsp, wbr, wout)
    cast_specs = [_cast_block_spec(*w.shape, n_seq, bsz * n_seq) for w in (wup_f32, wdn_f32)]
    return pl.pallas_call(
        functools.partial(_mixer_kernel, layer, n_layers),
        grid=(bsz, n_seq),
        in_specs=[pl.BlockSpec((1, ts, d), lambda b, s: (b, s, 0)),
                  pl.BlockSpec((1, XA_WIDTH, mlen), lambda b, s: (b, 0, 0)),
                  pl.BlockSpec((1, mlen, XA_WIDTH), lambda b, s: (b, 0, 0))]
                 + [_const_spec(c.shape) for c in consts] + cast_specs,
        out_specs=[pl.BlockSpec((1, ts, d), lambda b, s: (b, s, 0))] + cast_specs,
        out_shape=[jax.ShapeDtypeStruct(x.shape, F32),
                   jax.ShapeDtypeStruct(wup_f32.shape, BF16),
                   jax.ShapeDtypeStruct(wdn_f32.shape, BF16)],
        scratch_shapes=[pltpu.VMEM((HG_HEADS, HG_DIM, HG_DIM), F32),
                        pltpu.VMEM((N_BRANCH, ROW_GROUP, BRANCH_WIDTH), BF16),
                        pltpu.VMEM((ROW_GROUP, HG_WIDTH), F32)],
        compiler_params=pltpu.CompilerParams(
            dimension_semantics=("arbitrary", "arbitrary"),
            vmem_limit_bytes=V7X_VMEM_LIMIT_BYTES),
        name="mixer",
    )(x, kt, v, *consts, wup_f32, wdn_f32)


def _ffn_call(final, x, g2, wup, cw, cb, wdn, gf):
    bsz, seq, d = x.shape
    dff = wdn.shape[0]
    ts = SEQ_TILE
    consts = (g2, wup, cw, cb, wdn, gf)
    return pl.pallas_call(
        functools.partial(_ffn_kernel, final),
        grid=(bsz, seq // ts),
        in_specs=[pl.BlockSpec((1, ts, d), lambda b, s: (b, s, 0))]
                 + [_const_spec(c.shape) for c in consts],
        out_specs=pl.BlockSpec((1, ts, d), lambda b, s: (b, s, 0)),
        out_shape=jax.ShapeDtypeStruct(x.shape, F32),
        scratch_shapes=[pltpu.VMEM((F32_ROWS, dff), F32)],
        compiler_params=pltpu.CompilerParams(
            dimension_semantics=("parallel", "arbitrary"),
            vmem_limit_bytes=V7X_VMEM_LIMIT_BYTES),
        name="ffn",
    )(x, *consts)


def kernel(x, mem, norm1_g, w_in, ln_v_g, ln_v_b, w_spatial, b_spatial, lb_logits, hgrn_norm_g,
           mem_norm_g, w_mem_kv, w_branch, w_out, norm2_g, w_up, conv_w, conv_b, w_down, final_g):
    depth = w_in.shape[0]
    assert x.shape[1] % SEQ_TILE == 0 and SEQ_TILE % ROW_GROUP == 0
    assert ROW_GROUP % GM_CHUNK == 0 and ROW_GROUP % HG_CHUNK == 0
    assert HG_CHUNK % HG_SUB == 0 and HG_SUB % 16 == 0
    row = lambda p: p.reshape(1, -1)
    gd = GM_WIDTH // GM_GROUPS
    d = x.shape[-1]
    sections = ((GM_WIDTH, 0.5), (GM_WIDTH, 0.5), (HG_WIDTH, 0.5), (HG_WIDTH, 0.5), (HG_WIDTH, 1.0),
                (HG_WIDTH, 0.5), (XA_WIDTH, 1.0), (N_BRANCH * d, 0.5))
    in_scale = jnp.concatenate([jnp.full((1, w), s, F32) for w, s in sections], axis=1)
    for l in range(depth):
        bsp = jnp.repeat(b_spatial[l].T, gd, axis=1)
        wide = lambda p: jnp.pad(p, ((0, 0), (0, d - p.shape[-1])))
        params = jnp.concatenate(
            [row(norm1_g[l]), row(mem_norm_g[l]),
             jnp.concatenate([row(ln_v_g[l]), row(ln_v_b[l])], axis=1),
             wide(row(hgrn_norm_g[l])), wide(lb_logits)], axis=0)
        params = jnp.pad(params, ((0, -params.shape[0] % F32_ROWS), (0, 0)))
        kt, v = _memkv_call(mem, params, w_mem_kv[l].astype(BF16))
        x, wup, wdn = _mixer_call(
            l, depth, x, kt, v, params, (w_in[l] * in_scale).astype(BF16), w_spatial[l].astype(BF16), bsp,
            (0.5 * w_branch[l]).astype(BF16), w_out[l].astype(BF16), w_up[l], w_down[l])
        x = _ffn_call(l == depth - 1, x, row(norm2_g[l]), wup, 0.5 * conv_w[l],
                      row(0.5 * conv_b[l]), wdn, row(final_g))
    return x
```
